```python
import math
import jax, jax.numpy as jnp
from jax import lax
import numpy as np

D_MODEL = 1024
BATCH = 4
SEQ = 8192
DEPTH = 1

POOL_WINDOWS = (2, 4, 8, 16)
POOL_GROUPS = len(POOL_WINDOWS)
POOL_WIDTH = D_MODEL
POOL_GROUP_DIM = POOL_WIDTH // POOL_GROUPS
SSM_EXPAND = 2
SSM_INNER = SSM_EXPAND * D_MODEL
SSM_HEAD_DIM = 64
SSM_HEADS = SSM_INNER // SSM_HEAD_DIM
SSM_GROUPS = 8
SSM_HEADS_PER_GROUP = SSM_HEADS // SSM_GROUPS
SSM_STATE = 128
SSM_CONV = 4
SSM_CHUNK = 128
SSM_CONV_DIM = SSM_INNER + 2 * SSM_GROUPS * SSM_STATE
SSM_NORM_GROUP = SSM_INNER // SSM_GROUPS
N_BRANCHES = 2
SPLITS = (
    POOL_WIDTH,
    POOL_WIDTH + SSM_INNER,
    POOL_WIDTH + SSM_INNER + SSM_CONV_DIM,
    POOL_WIDTH + SSM_INNER + SSM_CONV_DIM + SSM_HEADS,
    POOL_WIDTH + SSM_INNER + SSM_CONV_DIM + SSM_HEADS + D_MODEL,
)
IN_COLS = POOL_WIDTH + SSM_INNER + SSM_CONV_DIM + SSM_HEADS + N_BRANCHES * D_MODEL
PEER_HEADS = 8
PEER_KEYS = 128
PEER_EXPERTS = PEER_KEYS * PEER_KEYS
PEER_QUERY = 256
PEER_HALF = PEER_QUERY // 2
PEER_TOPK = 16
PEER_BLOCK = 128
DN_ALPHA = (2.0 * DEPTH) ** 0.25
DN_BETA = (8.0 * DEPTH) ** -0.25
LN_EPS = 1e-5
RMS_EPS = 1e-5

kernel_name = "hybrid_pool_ssd_peer_deepnorm"


def layer_norm(x, g, b):
    xf = x.astype(jnp.float32)
    mu = jnp.mean(xf, axis=-1, keepdims=True)
    xc = xf - mu
    var = jnp.mean(xc * xc, axis=-1, keepdims=True)
    return (xc * lax.rsqrt(var + LN_EPS) * g.astype(jnp.float32) + b.astype(jnp.float32)).astype(x.dtype)


def pool_mixer(u, pool_w, pool_scale):
    b, l, _ = u.shape
    uf = u.astype(jnp.float32).reshape(b, l, POOL_GROUPS, POOL_GROUP_DIM)
    cs = jnp.cumsum(uf, axis=1)
    pos = jnp.arange(1, l + 1, dtype=jnp.float32)
    means = []
    for gi, w in enumerate(POOL_WINDOWS):
        c = cs[:, :, gi]
        lagged = jnp.pad(c, ((0, 0), (w, 0), (0, 0)))[:, :l]
        count = jnp.minimum(pos, float(w))[None, :, None]
        means.append((c - lagged) / count)
    pooled = jnp.stack(means, axis=2) - uf
    mixed = jnp.einsum('blgc,gcd->blgd', pooled, pool_w.astype(jnp.float32))
    return (mixed.reshape(b, l, POOL_WIDTH) * pool_scale.astype(jnp.float32)).astype(u.dtype)


def causal_depthwise_conv(v, w, bias):
    out = lax.conv_general_dilated(
        v, w[:, None, :], window_strides=(1,), padding=[(SSM_CONV - 1, 0)],
        dimension_numbers=('NWC', 'WIO', 'NWC'), feature_group_count=v.shape[-1])
    return out + bias


def seg_decay(acum):
    q = acum.shape[-1]
    mask = jnp.tril(jnp.ones((q, q), dtype=bool))
    diff = acum[..., :, None] - acum[..., None, :]
    return jnp.exp(jnp.where(mask, diff, -jnp.inf))


def ssd_scan(x, dt, a, bmat, cmat):
    b, l = x.shape[0], x.shape[1]
    nc = l // SSM_CHUNK
    G, R, P, N, Q = SSM_GROUPS, SSM_HEADS_PER_GROUP, SSM_HEAD_DIM, SSM_STATE, SSM_CHUNK
    xd = (x.astype(jnp.float32) * dt[..., None]).reshape(b, nc, Q, G, R, P)
    adt = (dt * a).reshape(b, nc, Q, G, R)
    acum = jnp.cumsum(adt, axis=2).transpose(0, 1, 3, 4, 2)
    bc = bmat.astype(jnp.float32).reshape(b, nc, Q, G, N)
    cc = cmat.astype(jnp.float32).reshape(b, nc, Q, G, N)
    lmat = seg_decay(acum)
    cb = jnp.einsum('bclgn,bcsgn->bcgls', cc, bc)
    y_diag = jnp.einsum('bcgls,bcgrls,bcsgrp->bclgrp', cb, lmat, xd)
    decay_states = jnp.exp(acum[..., -1:] - acum)
    states = jnp.einsum('bclgn,bcgrl,bclgrp->bcgrpn', bc, decay_states, xd)
    chunk_decay = jnp.exp(acum[..., -1])

    def step(h, inp):
        st, dc = inp
        return h * dc[..., None, None] + st, h

    h0 = jnp.zeros((b, G, R, P, N), jnp.float32)
    _, prev = lax.scan(step, h0, (jnp.moveaxis(states, 1, 0), jnp.moveaxis(chunk_decay, 1, 0)))
    prev = jnp.moveaxis(prev, 0, 1)
    y_off = jnp.einsum('bclgn,bcgrpn,bcgrl->bclgrp', cc, prev, jnp.exp(acum))
    return (y_diag + y_off).reshape(b, l, SSM_HEADS, P)


def ssm_mixer(z, xbc, dt_raw, conv_w, conv_b, dt_bias, a_log, d_skip, norm_w, out_w):
    b, l, _ = z.shape
    xbc = jax.nn.silu(causal_depthwise_conv(xbc, conv_w, conv_b))
    xs = xbc[..., :SSM_INNER].reshape(b, l, SSM_HEADS, SSM_HEAD_DIM)
    bm = xbc[..., SSM_INNER:SSM_INNER + SSM_GROUPS * SSM_STATE].reshape(b, l, SSM_GROUPS, SSM_STATE)
    cm = xbc[..., SSM_INNER + SSM_GROUPS * SSM_STATE:].reshape(b, l, SSM_GROUPS, SSM_STATE)
    dt = jax.nn.softplus(dt_raw.astype(jnp.float32) + dt_bias.astype(jnp.float32))
    a = -jnp.exp(a_log.astype(jnp.float32))
    y = ssd_scan(xs, dt, a, bm, cm)
    y = y + d_skip.astype(jnp.float32)[:, None] * xs.astype(jnp.float32)
    y = y.reshape(b, l, SSM_INNER) * jax.nn.silu(z.astype(jnp.float32))
    yg = y.reshape(b, l, SSM_GROUPS, SSM_NORM_GROUP)
    yg = yg * lax.rsqrt(jnp.mean(yg * yg, axis=-1, keepdims=True) + RMS_EPS)
    y = (yg.reshape(b, l, SSM_INNER) * norm_w.astype(jnp.float32)).astype(z.dtype)
    return y @ out_w


def peer(x, w_q, sub_keys, expert_u, expert_v):
    b, l, d = x.shape
    t = b * l
    xt = x.reshape(t, d)
    q = (xt @ w_q).reshape(t, PEER_HEADS, 2, PEER_HALF).astype(jnp.float32)
    s = jnp.einsum('thkc,knc->thkn', q, sub_keys.astype(jnp.float32))
    top_s, top_i = lax.top_k(s, PEER_TOPK)
    cand_s = top_s[:, :, 0, :, None] + top_s[:, :, 1, None, :]
    cand_i = top_i[:, :, 0, :, None] * PEER_KEYS + top_i[:, :, 1, None, :]
    best_s, best_pos = lax.top_k(cand_s.reshape(t, PEER_HEADS, PEER_TOPK * PEER_TOPK), PEER_TOPK)
    expert_idx = jnp.take_along_axis(cand_i.reshape(t, PEER_HEADS, PEER_TOPK * PEER_TOPK), best_pos, axis=-1)
    gate = jax.nn.softmax(best_s, axis=-1)
    nb = t // PEER_BLOCK

    def block(args):
        xb, ib, gb = args
        h = jnp.einsum('td,thkd->thk', xb, expert_u[ib])
        act = (jax.nn.gelu(h.astype(jnp.float32), approximate=False) * gb).astype(xb.dtype)
        return jnp.einsum('thk,thkd->td', act, expert_v[ib])

    out = lax.map(block, (xt.reshape(nb, PEER_BLOCK, d),
                          expert_idx.reshape(nb, PEER_BLOCK, PEER_HEADS, PEER_TOPK),
                          gate.reshape(nb, PEER_BLOCK, PEER_HEADS, PEER_TOPK)))
    return out.reshape(b, l, d)


def setup_inputs(seed: int = 0) -> dict:
    key = jax.random.key(seed)
    ks = jax.random.split(key, 20)
    f32 = jnp.float32
    nrm = lambda k, shape, scale: jax.random.normal(k, shape, f32) * scale
    dt0 = jnp.exp(jax.random.uniform(ks[6], (DEPTH, SSM_HEADS), f32)
                  * (math.log(0.1) - math.log(0.001)) + math.log(0.001))
    return {
        "x": nrm(ks[0], (BATCH, SEQ, D_MODEL), 1.0),
        "w_in": nrm(ks[1], (DEPTH, D_MODEL, IN_COLS), D_MODEL ** -0.5),
        "pool_w": nrm(ks[2], (DEPTH, POOL_GROUPS, POOL_GROUP_DIM, POOL_GROUP_DIM), POOL_GROUP_DIM ** -0.5),
        "pool_scale": 1.0 + nrm(ks[3], (DEPTH, POOL_WIDTH), 0.02),
        "conv_w": nrm(ks[4], (DEPTH, SSM_CONV, SSM_CONV_DIM), SSM_CONV ** -0.5),
        "conv_b": nrm(ks[5], (DEPTH, SSM_CONV_DIM), 0.02),
        "dt_bias": dt0 + jnp.log(-jnp.expm1(-dt0)),
        "a_log": jnp.log(jax.random.uniform(ks[7], (DEPTH, SSM_HEADS), f32, 1.0, 16.0)),
        "d_skip": 1.0 + nrm(ks[8], (DEPTH, SSM_HEADS), 0.02),
        "ssm_norm_w": 1.0 + nrm(ks[9], (DEPTH, SSM_INNER), 0.02),
        "ssm_out": nrm(ks[10], (DEPTH, SSM_INNER, D_MODEL), SSM_INNER ** -0.5),
        "w_out": nrm(ks[11], (DEPTH, D_MODEL, D_MODEL), D_MODEL ** -0.5 * DN_BETA),
        "ln1_g": 1.0 + nrm(ks[12], (DEPTH, D_MODEL), 0.02),
        "ln1_b": nrm(ks[13], (DEPTH, D_MODEL), 0.02),
        "w_q": nrm(ks[14], (DEPTH, D_MODEL, PEER_HEADS * PEER_QUERY), D_MODEL ** -0.5),
        "sub_keys": nrm(ks[15], (DEPTH, 2, PEER_KEYS, PEER_HALF), PEER_HALF ** -0.5),
        "expert_u": nrm(ks[16], (DEPTH, PEER_EXPERTS, D_MODEL), D_MODEL ** -0.5),
        "expert_v": nrm(ks[17], (DEPTH, PEER_EXPERTS, D_MODEL), DN_BETA),
        "ln2_g": 1.0 + nrm(ks[18], (DEPTH, D_MODEL), 0.02),
        "ln2_b": nrm(ks[19], (DEPTH, D_MODEL), 0.02),
    }


def reference(x, w_in, pool_w, pool_scale, conv_w, conv_b, dt_bias, a_log, d_skip, ssm_norm_w,
              ssm_out, w_out, ln1_g, ln1_b, w_q, sub_keys, expert_u, expert_v, ln2_g, ln2_b):
    h = x
    for i in range(DEPTH):
        proj = h @ w_in[i]
        pool_in, z, xbc, dt_raw, g_pool, g_ssm = jnp.split(proj, SPLITS, axis=-1)
        a_out = pool_mixer(pool_in, pool_w[i], pool_scale[i])
        m_out = ssm_mixer(z, xbc, dt_raw, conv_w[i], conv_b[i], dt_bias[i], a_log[i],
                          d_skip[i], ssm_norm_w[i], ssm_out[i])
        merged = jax.nn.sigmoid(g_pool) * a_out + jax.nn.sigmoid(g_ssm) * m_out
        h = layer_norm(DN_ALPHA * h + merged @ w_out[i], ln1_g[i], ln1_b[i])
        h = layer_norm(DN_ALPHA * h + peer(h, w_q[i], sub_keys[i], expert_u[i], expert_v[i]),
                       ln2_g[i], ln2_b[i])
    return h
```

```python
import functools
import math

import numpy as np
import jax
import jax.numpy as jnp
from jax import lax
from jax.experimental import pallas as pl
from jax.experimental.pallas import tpu as pltpu

F32 = jnp.float32
BF16 = jnp.bfloat16

D_MODEL = 1024
POOL_WINDOWS = (2, 4, 8, 16)
POOL_GROUPS = len(POOL_WINDOWS)
POOL_GROUP_DIM = D_MODEL // POOL_GROUPS
POOL_HISTORY = 16
SSM_INNER = 2 * D_MODEL
SSM_HEAD_DIM = 64
SSM_HEADS = SSM_INNER // SSM_HEAD_DIM
SSM_GROUPS = 8
SSM_HEADS_PER_GROUP = SSM_HEADS // SSM_GROUPS
SSM_STATE = 128
SSM_CONV = 4
SSM_CHUNK = 128
SSM_BC = SSM_GROUPS * SSM_STATE
SSM_CONV_DIM = SSM_INNER + 2 * SSM_BC
SSM_NORM_GROUP = SSM_INNER // SSM_GROUPS
SSM_GROUP_COLS = SSM_HEADS_PER_GROUP * SSM_HEAD_DIM
CONV_HISTORY = 8
PEER_HEADS = 8
PEER_KEYS = 128
PEER_HALF = 128
PEER_TOPK = 16
DEPTH = 1
DN_ALPHA = (2.0 * DEPTH) ** 0.25
LN_EPS = 1e-5
RMS_EPS = 1e-5

LANES = 128
VMEM_LIMIT_BYTES = 56 * 1024 * 1024

MIXER_TOKENS = 256
XBC_COL_BLOCK = 512
ROUTE_TOKENS = 256
EXPERT_TOKENS = 512
EXPERT_ROWS = 8

_CANDIDATES = [(a, b) for a in range(PEER_TOPK) for b in range(PEER_TOPK)
               if (a + 1) * (b + 1) <= PEER_TOPK]


def _dot(a, b):
    return jnp.dot(a, b, preferred_element_type=F32)


def _split_dot_right(v, onehot, terms):
    acc = None
    rem = v
    for _ in range(terms):
        part = rem.astype(BF16)
        d = _dot(part, onehot)
        acc = d if acc is None else acc + d
        rem = rem - part.astype(F32)
    return acc


def _split_dot_left(onehot, v, terms):
    acc = None
    rem = v
    for _ in range(terms):
        part = rem.astype(BF16)
        d = _dot(onehot, part)
        acc = d if acc is None else acc + d
        rem = rem - part.astype(F32)
    return acc


def _sigmoid(x):
    return 1.0 / (1.0 + jnp.exp(-x))


def _silu(x):
    return x * _sigmoid(x)


def _layer_norm(x, g, b):
    mu = jnp.mean(x, axis=-1, keepdims=True)
    xc = x - mu
    var = jnp.mean(xc * xc, axis=-1, keepdims=True)
    return xc * lax.rsqrt(var + LN_EPS) * g + b


def _mixer_kernel(x_ref, wpool_ref, wz_ref, wxbc_ref, wdt_ref, wgp_ref, wgs_ref,
                  poolw_ref, pscale_ref, convw_ref, convb_ref, dtb_ref, alog_ref, dskip_ref,
                  normw_ref, ssmout_ref, wout_ref, ln1g_ref, ln1b_ref,
                  ltri_ref, ehead_ref, ecol_ref,
                  out_ref,
                  pool_ext, xbc_ext, xbc_act, y_scr, a_scr, state):
    tq = x_ref.shape[1]
    s_idx = pl.program_id(1)

    @pl.when(s_idx == 0)
    def _():
        pool_ext[0:POOL_HISTORY, :] = jnp.zeros((POOL_HISTORY, D_MODEL), F32)
        xbc_ext[0:CONV_HISTORY, :] = jnp.zeros((CONV_HISTORY, SSM_CONV_DIM), F32)
        state[...] = jnp.zeros(state.shape, F32)

    x = x_ref[0]
    xb = x.astype(BF16)

    pool_ext[POOL_HISTORY:POOL_HISTORY + tq, :] = _dot(xb, wpool_ref[...])
    pos = (s_idx * tq + lax.broadcasted_iota(jnp.int32, (tq, 1), 0) + 1).astype(F32)
    for g, w in enumerate(POOL_WINDOWS):
        cols = slice(g * POOL_GROUP_DIM, (g + 1) * POOL_GROUP_DIM)
        u = pool_ext[POOL_HISTORY:POOL_HISTORY + tq, cols]
        acc = u
        for k in range(1, w):
            acc = acc + pool_ext[POOL_HISTORY - k:POOL_HISTORY - k + tq, cols]
        pooled = acc / jnp.minimum(pos, float(w)) - u
        mixed = _dot(pooled.astype(BF16), poolw_ref[g])
        a_scr[:, cols] = mixed * pscale_ref[:, cols]
    pool_ext[0:POOL_HISTORY, :] = pool_ext[tq:tq + POOL_HISTORY, :]

    for cb in range(SSM_CONV_DIM // XBC_COL_BLOCK):
        cols = slice(cb * XBC_COL_BLOCK, (cb + 1) * XBC_COL_BLOCK)
        xbc_ext[CONV_HISTORY:CONV_HISTORY + tq, cols] = _dot(xb, wxbc_ref[:, cols])
        conv = convb_ref[:, cols]
        for k in range(SSM_CONV):
            r0 = CONV_HISTORY - (SSM_CONV - 1) + k
            conv = conv + convw_ref[k:k + 1, cols] * xbc_ext[r0:r0 + tq, cols]
        xbc_act[:, cols] = _silu(conv)
    xbc_ext[0:CONV_HISTORY, :] = xbc_ext[tq:tq + CONV_HISTORY, :]

    dtr = _dot(xb, wdt_ref[...]) + dtb_ref[...]
    dt = jnp.maximum(dtr, 0.0) + jnp.log(1.0 + jnp.exp(-jnp.abs(dtr)))
    adt = dt * (-jnp.exp(alog_ref[...]))

    tri = (lax.broadcasted_iota(jnp.int32, (SSM_CHUNK, SSM_CHUNK), 0)
           >= lax.broadcasted_iota(jnp.int32, (SSM_CHUNK, SSM_CHUNK), 1))
    ltri = ltri_ref[...]
    ehead = ehead_ref[...]
    ecol = ecol_ref[...]
    x_off, b_off, c_off = 0, SSM_INNER, SSM_INNER + SSM_BC

    for c in range(tq // SSM_CHUNK):
        rows = slice(c * SSM_CHUNK, (c + 1) * SSM_CHUNK)
        acum = _split_dot_left(ltri, adt[rows], 3)
        acum_t = acum.T
        alast = acum[SSM_CHUNK - 1:SSM_CHUNK, :]
        stack = jnp.concatenate([dt[rows], jnp.exp(acum), jnp.exp(alast - acum)], axis=0)
        expanded = _split_dot_right(stack, ehead, 2)
        dt_e = expanded[0:SSM_CHUNK]
        ea_e = expanded[SSM_CHUNK:2 * SSM_CHUNK]
        dec_e = expanded[2 * SSM_CHUNK:3 * SSM_CHUNK]
        colb = _split_dot_right(acum, ecol, 3)

        for g in range(SSM_GROUPS):
            gcols = slice(g * SSM_GROUP_COLS, (g + 1) * SSM_GROUP_COLS)
            ncols_b = slice(b_off + g * SSM_STATE, b_off + (g + 1) * SSM_STATE)
            ncols_c = slice(c_off + g * SSM_STATE, c_off + (g + 1) * SSM_STATE)
            cg = xbc_act[rows, ncols_c].astype(BF16)
            bg_t = xbc_act[rows, ncols_b].T.astype(BF16)
            cb_mat = _dot(cg, bg_t)
            xs_g = xbc_act[rows, x_off + g * SSM_GROUP_COLS:x_off + (g + 1) * SSM_GROUP_COLS]
            xd_g = xs_g * dt_e[:, gcols]
            y_parts = []
            for r in range(SSM_HEADS_PER_GROUP):
                hd = g * SSM_HEADS_PER_GROUP + r
                diff = colb[:, hd * SSM_CHUNK:(hd + 1) * SSM_CHUNK] - acum_t[hd:hd + 1, :]
                lmat = jnp.where(tri, jnp.exp(diff), 0.0)
                gmat = (cb_mat * lmat).astype(BF16)
                xd_r = xd_g[:, r * SSM_HEAD_DIM:(r + 1) * SSM_HEAD_DIM].astype(BF16)
                y_parts.append(_dot(gmat, xd_r))
            y_diag = jnp.concatenate(y_parts, axis=1)
            st = state[g]
            y_off = _dot(cg, st.astype(BF16)) * ea_e[:, gcols]
            y_scr[rows, gcols] = y_diag + y_off + dskip_ref[:, gcols] * xs_g
            upd = _dot(bg_t, (xd_g * dec_e[:, gcols]).astype(BF16))
            state[g] = st * ea_e[SSM_CHUNK - 1:SSM_CHUNK, gcols] + upd

    y = y_scr[...] * _silu(_dot(xb, wz_ref[...]))
    for g in range(SSM_GROUPS):
        ncols = slice(g * SSM_NORM_GROUP, (g + 1) * SSM_NORM_GROUP)
        yg = y[:, ncols]
        ms = jnp.mean(yg * yg, axis=-1, keepdims=True)
        y_scr[:, ncols] = yg * lax.rsqrt(ms + RMS_EPS) * normw_ref[:, ncols]
    m_out = _dot(y_scr[...].astype(BF16), ssmout_ref[...])

    merged = (_sigmoid(_dot(xb, wgp_ref[...])) * a_scr[...]
              + _sigmoid(_dot(xb, wgs_ref[...])) * m_out)
    mix = _dot(merged.astype(BF16), wout_ref[...])
    out_ref[0] = _layer_norm(DN_ALPHA * x + mix, ln1g_ref[...], ln1b_ref[...])


def _resident(shape):
    zeros = (0,) * len(shape)
    return pl.BlockSpec(shape, lambda b, s: zeros, pipeline_mode=pl.Buffered(1))


def _mixer(x, w_in, pool_w, pool_scale, conv_w, conv_b, dt_bias, a_log, d_skip, ssm_norm_w,
           ssm_out, w_out, ln1_g, ln1_b):
    bsz, seq, _ = x.shape
    tq = min(MIXER_TOKENS, seq)
    assert seq % tq == 0 and tq % SSM_CHUNK == 0
    o1 = D_MODEL
    o2 = o1 + SSM_INNER
    o3 = o2 + SSM_CONV_DIM
    o4 = o3 + SSM_HEADS
    o5 = o4 + D_MODEL
    pad_h = LANES - SSM_HEADS
    w_pool = w_in[:, :o1].astype(BF16)
    w_z = w_in[:, o1:o2].astype(BF16)
    w_xbc = w_in[:, o2:o3].astype(BF16)
    w_dt = jnp.pad(w_in[:, o3:o4], ((0, 0), (0, pad_h))).astype(BF16)
    w_gp = w_in[:, o4:o5].astype(BF16)
    w_gs = w_in[:, o5:].astype(BF16)
    row = lambda v: v.reshape(1, -1).astype(F32)
    dtb = jnp.pad(row(dt_bias), ((0, 0), (0, pad_h)))
    alog = jnp.pad(row(a_log), ((0, 0), (0, pad_h)))
    dskip = jnp.repeat(row(d_skip), SSM_HEAD_DIM, axis=1)

    q = SSM_CHUNK
    ltri = jnp.asarray(np.tril(np.ones((q, q), np.float32)), BF16)
    eh = np.zeros((LANES, SSM_INNER), np.float32)
    ec = np.zeros((LANES, SSM_HEADS * q), np.float32)
    for h in range(SSM_HEADS):
        eh[h, h * SSM_HEAD_DIM:(h + 1) * SSM_HEAD_DIM] = 1.0
        ec[h, h * q:(h + 1) * q] = 1.0
    ehead = jnp.asarray(eh, BF16)
    ecol = jnp.asarray(ec, BF16)

    operands = [
        x, w_pool, w_z, w_xbc, w_dt, w_gp, w_gs,
        pool_w.astype(BF16), row(pool_scale), conv_w.astype(F32), row(conv_b), dtb, alog, dskip,
        row(ssm_norm_w), ssm_out.astype(BF16), w_out.astype(BF16), row(ln1_g), row(ln1_b),
        ltri, ehead, ecol,
    ]
    in_specs = [pl.BlockSpec((1, tq, D_MODEL), lambda b, s: (b, s, 0))]
    in_specs += [_resident(op.shape) for op in operands[1:]]
    return pl.pallas_call(
        _mixer_kernel,
        grid=(bsz, seq // tq),
        in_specs=in_specs,
        out_specs=pl.BlockSpec((1, tq, D_MODEL), lambda b, s: (b, s, 0)),
        out_shape=jax.ShapeDtypeStruct(x.shape, F32),
        scratch_shapes=[
            pltpu.VMEM((POOL_HISTORY + tq, D_MODEL), F32),
            pltpu.VMEM((CONV_HISTORY + tq, SSM_CONV_DIM), F32),
            pltpu.VMEM((tq, SSM_CONV_DIM), F32),
            pltpu.VMEM((tq, SSM_INNER), F32),
            pltpu.VMEM((tq, D_MODEL), F32),
            pltpu.VMEM((SSM_GROUPS, SSM_STATE, SSM_GROUP_COLS), F32),
        ],
        compiler_params=pltpu.CompilerParams(
            dimension_semantics=("arbitrary", "arbitrary"),
            vmem_limit_bytes=VMEM_LIMIT_BYTES),
        name="mixer",
    )(*operands)


def _top16_rounds(scores, key_iota):
    cur = scores
    rank = jnp.full(scores.shape, float(PEER_TOPK), F32)
    tops = []
    for r in range(PEER_TOPK):
        m = jnp.max(cur, axis=0, keepdims=True)
        first = jnp.min(jnp.where(cur == m, key_iota, float(PEER_KEYS)), axis=0, keepdims=True)
        hit = key_iota == first
        cur = jnp.where(hit, -jnp.inf, cur)
        rank = jnp.where(hit, float(r), rank)
        tops.append(m)
    return rank, tops


def _route_kernel(h_ref, wqt_ref, keys_ref,
                  xt_ref, cnt_ref, e1_ref, rank2_ref, e2_ref,
                  qt_scr, rank1_scr, e1_scr, tops_scr, csel_scr):
    tr = h_ref.shape[0]
    xt = h_ref[...].T.astype(BF16)
    xt_ref[...] = xt
    qt_scr[...] = _dot(wqt_ref[...], xt)

    key_iota = lax.broadcasted_iota(jnp.int32, (PEER_KEYS, LANES), 0).astype(F32)
    head_iota = lax.broadcasted_iota(jnp.int32, (PEER_HEADS, LANES), 0)
    tops_scr[...] = jnp.zeros(tops_scr.shape, F32)

    def head_body(h, carry):
        for k in range(2):
            q0 = pl.multiple_of((h * 2 + k) * PEER_HALF, PEER_HALF)
            qk = qt_scr[pl.ds(q0, PEER_HALF), :].astype(BF16)
            scores = _dot(keys_ref[k], qk)
            for lt in range(tr // LANES):
                lanes = slice(lt * LANES, (lt + 1) * LANES)
                sc = scores[:, lanes]
                rank, tops = _top16_rounds(sc, key_iota)
                e = jnp.exp(sc - tops[0])
                if k == 0:
                    rank1_scr[h, :, lanes] = rank
                    e1_scr[h, :, lanes] = e
                else:
                    rank2_ref[h, :, lanes] = rank.astype(BF16)
                    e2_ref[h, :, lanes] = e.astype(BF16)
                for r in range(PEER_TOPK):
                    old = tops_scr[k, r, :, lanes]
                    tops_scr[k, r, :, lanes] = jnp.where(head_iota == h, tops[r], old)
        return carry

    lax.fori_loop(0, PEER_HEADS, head_body, 0)

    for lt in range(tr // LANES):
        lanes = slice(lt * LANES, (lt + 1) * LANES)
        s1 = [tops_scr[0, a, :, lanes] for a in range(PEER_TOPK)]
        s2 = [tops_scr[1, b, :, lanes] for b in range(PEER_TOPK)]
        sums = {ab: s1[ab[0]] + s2[ab[1]] for ab in _CANDIDATES}
        p1 = [jnp.exp(s1[a] - s1[0]) for a in range(PEER_TOPK)]
        p2 = [jnp.exp(s2[b] - s2[0]) for b in range(PEER_TOPK)]
        counts = [jnp.zeros((PEER_HEADS, LANES), F32) for _ in range(PEER_TOPK)]
        z = jnp.zeros((PEER_HEADS, LANES), F32)
        for (a, b) in _CANDIDATES:
            s = sums[(a, b)]
            beaten_by = jnp.full((PEER_HEADS, LANES), float((a + 1) * (b + 1) - 1), F32)
            for (a2, b2) in _CANDIDATES:
                if a2 < a and b2 > b:
                    beaten_by = beaten_by + jnp.where(sums[(a2, b2)] >= s, 1.0, 0.0)
                elif a2 > a and b2 < b:
                    beaten_by = beaten_by + jnp.where(sums[(a2, b2)] > s, 1.0, 0.0)
            sel = beaten_by < float(PEER_TOPK)
            counts[a] = counts[a] + jnp.where(sel, 1.0, 0.0)
            z = z + jnp.where(sel, p1[a] * p2[b], 0.0)
        for a in range(PEER_TOPK):
            csel_scr[a, :, lanes] = counts[a]
        csel_scr[PEER_TOPK, :, lanes] = 1.0 / z

    for h in range(PEER_HEADS):
        for lt in range(tr // LANES):
            lanes = slice(lt * LANES, (lt + 1) * LANES)
            rank = rank1_scr[h, :, lanes]
            cnt = jnp.zeros((PEER_KEYS, LANES), F32)
            for a in range(PEER_TOPK):
                cnt = jnp.where(rank == float(a), csel_scr[a, h:h + 1, lanes], cnt)
            cnt_ref[h, :, lanes] = cnt
            e1_ref[h, :, lanes] = e1_scr[h, :, lanes] * csel_scr[PEER_TOPK, h:h + 1, lanes]


def _route(h1, w_q, sub_keys):
    t = h1.shape[0]
    tr = min(ROUTE_TOKENS, t)
    assert t % tr == 0 and tr % LANES == 0
    wqt = w_q.T.astype(BF16)
    keys = sub_keys.astype(BF16)
    nq = wqt.shape[0]
    table = lambda dt: jax.ShapeDtypeStruct((PEER_HEADS, PEER_KEYS, t), dt)
    table_spec = pl.BlockSpec((PEER_HEADS, PEER_KEYS, tr), lambda i: (0, 0, i))
    return pl.pallas_call(
        _route_kernel,
        grid=(t // tr,),
        in_specs=[
            pl.BlockSpec((tr, D_MODEL), lambda i: (i, 0)),
            pl.BlockSpec(wqt.shape, lambda i: (0, 0), pipeline_mode=pl.Buffered(1)),
            pl.BlockSpec(keys.shape, lambda i: (0, 0, 0), pipeline_mode=pl.Buffered(1)),
        ],
        out_specs=[
            pl.BlockSpec((D_MODEL, tr), lambda i: (0, i)),
            table_spec, table_spec, table_spec, table_spec,
        ],
        out_shape=[
            jax.ShapeDtypeStruct((D_MODEL, t), BF16),
            table(F32), table(F32), table(BF16), table(BF16),
        ],
        scratch_shapes=[
            pltpu.VMEM((nq, tr), F32),
            pltpu.VMEM((PEER_HEADS, PEER_KEYS, tr), F32),
            pltpu.VMEM((PEER_HEADS, PEER_KEYS, tr), F32),
            pltpu.VMEM((2, PEER_TOPK, PEER_HEADS, tr), F32),
            pltpu.VMEM((PEER_TOPK + 1, PEER_HEADS, tr), F32),
        ],
        compiler_params=pltpu.CompilerParams(
            dimension_semantics=("arbitrary",),
            vmem_limit_bytes=VMEM_LIMIT_BYTES),
        name="route",
    )(h1, wqt, keys)


def _expert_kernel(xt_ref, cnt_ref, e1_ref, rank2_ref, e2_ref, u_ref, vt_ref,
                   h1_ref, ln2g_ref, ln2b_ref,
                   out_ref,
                   acc_scr, act_scr):
    e_idx = pl.program_id(1)

    @pl.when(e_idx == 0)
    def _():
        acc_scr[...] = jnp.zeros(acc_scr.shape, F32)

    pre = _dot(u_ref[...], xt_ref[...])
    sqrt_half = math.sqrt(0.5)
    for i in range(EXPERT_ROWS):
        rows = slice(i * PEER_KEYS, (i + 1) * PEER_KEYS)
        hi = pre[rows]
        gelu = 0.5 * hi * (1.0 + lax.erf(hi * sqrt_half))
        w = None
        for h in range(PEER_HEADS):
            cnt = cnt_ref[h, i:i + 1, :].astype(BF16)
            e1 = e1_ref[h, i:i + 1, :].astype(BF16)
            term = jnp.where(rank2_ref[h] < cnt, e2_ref[h], jnp.zeros((), BF16)) * e1
            w = term if w is None else w + term
        act_scr[rows, :] = (gelu * w.astype(F32)).astype(BF16)
    acc_scr[...] += _dot(vt_ref[...], act_scr[...])

    @pl.when(e_idx == pl.num_programs(1) - 1)
    def _():
        res = DN_ALPHA * h1_ref[...] + acc_scr[...].T
        out_ref[...] = _layer_norm(res, ln2g_ref[...], ln2b_ref[...])


def _experts(h1, xt, cnt, e1, rank2, e2, expert_u, expert_v, ln2_g, ln2_b):
    t = h1.shape[0]
    tm = min(EXPERT_TOKENS, t)
    assert t % tm == 0
    n_exp = expert_u.shape[0]
    eb = EXPERT_ROWS * PEER_KEYS
    assert n_exp % eb == 0
    u = expert_u.astype(BF16)
    vt = expert_v.T.astype(BF16)
    tab_full = pl.BlockSpec((PEER_HEADS, PEER_KEYS, tm), lambda i, e: (0, 0, i))
    tab_rows = pl.BlockSpec((PEER_HEADS, EXPERT_ROWS, tm), lambda i, e: (0, e, i))
    vec = lambda v: v.reshape(1, -1).astype(F32)
    return pl.pallas_call(
        _expert_kernel,
        grid=(t // tm, n_exp // eb),
        in_specs=[
            pl.BlockSpec((D_MODEL, tm), lambda i, e: (0, i)),
            tab_rows, tab_rows, tab_full, tab_full,
            pl.BlockSpec((eb, D_MODEL), lambda i, e: (e, 0)),
            pl.BlockSpec((D_MODEL, eb), lambda i, e: (0, e)),
            pl.BlockSpec((tm, D_MODEL), lambda i, e: (i, 0)),
            pl.BlockSpec((1, D_MODEL), lambda i, e: (0, 0)),
            pl.BlockSpec((1, D_MODEL), lambda i, e: (0, 0)),
        ],
        out_specs=pl.BlockSpec((tm, D_MODEL), lambda i, e: (i, 0)),
        out_shape=jax.ShapeDtypeStruct((t, D_MODEL), F32),
        scratch_shapes=[
            pltpu.VMEM((D_MODEL, tm), F32),
            pltpu.VMEM((eb, tm), BF16),
        ],
        compiler_params=pltpu.CompilerParams(
            dimension_semantics=("arbitrary", "arbitrary"),
            vmem_limit_bytes=VMEM_LIMIT_BYTES),
        name="experts",
    )(xt, cnt, e1, rank2, e2, u, vt, h1, vec(ln2_g), vec(ln2_b))


def kernel(x, w_in, pool_w, pool_scale, conv_w, conv_b, dt_bias, a_log, d_skip, ssm_norm_w,
           ssm_out, w_out, ln1_g, ln1_b, w_q, sub_keys, expert_u, expert_v, ln2_g, ln2_b):
    bsz, seq, d = x.shape
    h = x
    for i in range(DEPTH):
        h1 = _mixer(h, w_in[i], pool_w[i], pool_scale[i], conv_w[i], conv_b[i], dt_bias[i],
                    a_log[i], d_skip[i], ssm_norm_w[i], ssm_out[i], w_out[i], ln1_g[i], ln1_b[i])
        h1 = h1.reshape(bsz * seq, d)
        xt, cnt, e1, rank2, e2 = _route(h1, w_q[i], sub_keys[i])
        h = _experts(h1, xt, cnt, e1, rank2, e2, expert_u[i], expert_v[i], ln2_g[i], ln2_b[i])
        h = h.reshape(bsz, seq, d)
    return h
```

```python
import functools
import math

import numpy as np
import jax
import jax.numpy as jnp
from jax import lax
from jax.experimental import pallas as pl
from jax.experimental.pallas import tpu as pltpu

F32 = jnp.float32
BF16 = jnp.bfloat16

D_MODEL = 1024
POOL_WINDOWS = (2, 4, 8, 16)
POOL_GROUPS = len(POOL_WINDOWS)
POOL_GROUP_DIM = D_MODEL // POOL_GROUPS
POOL_HISTORY = 16
SSM_INNER = 2 * D_MODEL
SSM_HEAD_DIM = 64
SSM_HEADS = SSM_INNER // SSM_HEAD_DIM
SSM_GROUPS = 8
SSM_HEADS_PER_GROUP = SSM_HEADS // SSM_GROUPS
SSM_STATE = 128
SSM_CONV = 4
SSM_CHUNK = 128
SSM_BC = SSM_GROUPS * SSM_STATE
SSM_CONV_DIM = SSM_INNER + 2 * SSM_BC
SSM_NORM_GROUP = SSM_INNER // SSM_GROUPS
SSM_GROUP_COLS = SSM_HEADS_PER_GROUP * SSM_HEAD_DIM
CONV_HISTORY = 8
PEER_HEADS = 8
PEER_KEYS = 128
PEER_HALF = 128
PEER_TOPK = 16
DEPTH = 1
DN_ALPHA = (2.0 * DEPTH) ** 0.25
LN_EPS = 1e-5
RMS_EPS = 1e-5

LANES = 128
VMEM_LIMIT_BYTES = 56 * 1024 * 1024

MIXER_TOKENS = 256
XBC_COL_BLOCK = 512
ROUTE_TOKENS = 256
EXPERT_TOKENS = 512
EXPERT_ROWS = 8
EXPERT_BLOCKS = PEER_KEYS // EXPERT_ROWS
EXPERT_LAST_IN_A = (EXPERT_BLOCKS - 1) % 2 == 0

_CANDIDATES = [(a, b) for a in range(PEER_TOPK) for b in range(PEER_TOPK)
               if (a + 1) * (b + 1) <= PEER_TOPK]


def _dot(a, b):
    return jnp.dot(a, b, preferred_element_type=F32)


def _split_dot_right(v, onehot, terms):
    acc = None
    rem = v
    for _ in range(terms):
        part = rem.astype(BF16)
        d = _dot(part, onehot)
        acc = d if acc is None else acc + d
        rem = rem - part.astype(F32)
    return acc


def _split_dot_left(onehot, v, terms):
    acc = None
    rem = v
    for _ in range(terms):
        part = rem.astype(BF16)
        d = _dot(onehot, part)
        acc = d if acc is None else acc + d
        rem = rem - part.astype(F32)
    return acc


def _sigmoid(x):
    return 1.0 / (1.0 + jnp.exp(-x))


def _silu(x):
    return x * _sigmoid(x)


def _layer_norm(x, g, b):
    mu = jnp.mean(x, axis=-1, keepdims=True)
    xc = x - mu
    var = jnp.mean(xc * xc, axis=-1, keepdims=True)
    return xc * lax.rsqrt(var + LN_EPS) * g + b


def _mixer_kernel(x_ref, wpool_ref, wz_ref, wxbc_ref, wdt_ref, wgp_ref, wgs_ref,
                  poolw_ref, pscale_ref, convw_ref, convb_ref, dtb_ref, alog_ref, dskip_ref,
                  normw_ref, ssmout_ref, wout_ref, ln1g_ref, ln1b_ref,
                  ltri_ref, ehead_ref, ecol_ref,
                  out_ref,
                  pool_ext, xbc_ext, xbc_act, y_scr, a_scr, state):
    tq = x_ref.shape[1]
    s_idx = pl.program_id(1)

    @pl.when(s_idx == 0)
    def _():
        pool_ext[0:POOL_HISTORY, :] = jnp.zeros((POOL_HISTORY, D_MODEL), F32)
        xbc_ext[0:CONV_HISTORY, :] = jnp.zeros((CONV_HISTORY, SSM_CONV_DIM), F32)
        state[...] = jnp.zeros(state.shape, F32)

    x = x_ref[0]
    xb = x.astype(BF16)

    pool_ext[POOL_HISTORY:POOL_HISTORY + tq, :] = _dot(xb, wpool_ref[...])
    pos = (s_idx * tq + lax.broadcasted_iota(jnp.int32, (tq, 1), 0) + 1).astype(F32)
    for g, w in enumerate(POOL_WINDOWS):
        cols = slice(g * POOL_GROUP_DIM, (g + 1) * POOL_GROUP_DIM)
        u = pool_ext[POOL_HISTORY:POOL_HISTORY + tq, cols]
        acc = u
        for k in range(1, w):
            acc = acc + pool_ext[POOL_HISTORY - k:POOL_HISTORY - k + tq, cols]
        pooled = acc / jnp.minimum(pos, float(w)) - u
        mixed = _dot(pooled.astype(BF16), poolw_ref[g])
        a_scr[:, cols] = mixed * pscale_ref[:, cols]
    pool_ext[0:POOL_HISTORY, :] = pool_ext[tq:tq + POOL_HISTORY, :]

    for cb in range(SSM_CONV_DIM // XBC_COL_BLOCK):
        cols = slice(cb * XBC_COL_BLOCK, (cb + 1) * XBC_COL_BLOCK)
        xbc_ext[CONV_HISTORY:CONV_HISTORY + tq, cols] = _dot(xb, wxbc_ref[:, cols])
        conv = convb_ref[:, cols]
        for k in range(SSM_CONV):
            r0 = CONV_HISTORY - (SSM_CONV - 1) + k
            conv = conv + convw_ref[k:k + 1, cols] * xbc_ext[r0:r0 + tq, cols]
        xbc_act[:, cols] = _silu(conv)
    xbc_ext[0:CONV_HISTORY, :] = xbc_ext[tq:tq + CONV_HISTORY, :]

    dtr = _dot(xb, wdt_ref[...]) + dtb_ref[...]
    dt = jnp.maximum(dtr, 0.0) + jnp.log(1.0 + jnp.exp(-jnp.abs(dtr)))
    adt = dt * (-jnp.exp(alog_ref[...]))

    tri = (lax.broadcasted_iota(jnp.int32, (SSM_CHUNK, SSM_CHUNK), 0)
           >= lax.broadcasted_iota(jnp.int32, (SSM_CHUNK, SSM_CHUNK), 1))
    ltri = ltri_ref[...]
    ehead = ehead_ref[...]
    ecol = ecol_ref[...]
    x_off, b_off, c_off = 0, SSM_INNER, SSM_INNER + SSM_BC

    for c in range(tq // SSM_CHUNK):
        rows = slice(c * SSM_CHUNK, (c + 1) * SSM_CHUNK)
        acum = _split_dot_left(ltri, adt[rows], 3)
        acum_t = acum.T
        alast = acum[SSM_CHUNK - 1:SSM_CHUNK, :]
        stack = jnp.concatenate([dt[rows], jnp.exp(acum), jnp.exp(alast - acum)], axis=0)
        expanded = _split_dot_right(stack, ehead, 2)
        dt_e = expanded[0:SSM_CHUNK]
        ea_e = expanded[SSM_CHUNK:2 * SSM_CHUNK]
        dec_e = expanded[2 * SSM_CHUNK:3 * SSM_CHUNK]
        colb = _split_dot_right(acum, ecol, 3)

        for g in range(SSM_GROUPS):
            gcols = slice(g * SSM_GROUP_COLS, (g + 1) * SSM_GROUP_COLS)
            ncols_b = slice(b_off + g * SSM_STATE, b_off + (g + 1) * SSM_STATE)
            ncols_c = slice(c_off + g * SSM_STATE, c_off + (g + 1) * SSM_STATE)
            cg = xbc_act[rows, ncols_c].astype(BF16)
            bg_t = xbc_act[rows, ncols_b].T.astype(BF16)
            cb_mat = _dot(cg, bg_t)
            xs_g = xbc_act[rows, x_off + g * SSM_GROUP_COLS:x_off + (g + 1) * SSM_GROUP_COLS]
            xd_g = xs_g * dt_e[:, gcols]
            y_parts = []
            for r in range(SSM_HEADS_PER_GROUP):
                hd = g * SSM_HEADS_PER_GROUP + r
                diff = colb[:, hd * SSM_CHUNK:(hd + 1) * SSM_CHUNK] - acum_t[hd:hd + 1, :]
                lmat = jnp.where(tri, jnp.exp(diff), 0.0)
                gmat = (cb_mat * lmat).astype(BF16)
                xd_r = xd_g[:, r * SSM_HEAD_DIM:(r + 1) * SSM_HEAD_DIM].astype(BF16)
                y_parts.append(_dot(gmat, xd_r))
            y_diag = jnp.concatenate(y_parts, axis=1)
            st = state[g]
            y_off = _dot(cg, st.astype(BF16)) * ea_e[:, gcols]
            y_scr[rows, gcols] = y_diag + y_off + dskip_ref[:, gcols] * xs_g
            upd = _dot(bg_t, (xd_g * dec_e[:, gcols]).astype(BF16))
            state[g] = st * ea_e[SSM_CHUNK - 1:SSM_CHUNK, gcols] + upd

    y = y_scr[...] * _silu(_dot(xb, wz_ref[...]))
    for g in range(SSM_GROUPS):
        ncols = slice(g * SSM_NORM_GROUP, (g + 1) * SSM_NORM_GROUP)
        yg = y[:, ncols]
        ms = jnp.mean(yg * yg, axis=-1, keepdims=True)
        y_scr[:, ncols] = yg * lax.rsqrt(ms + RMS_EPS) * normw_ref[:, ncols]
    m_out = _dot(y_scr[...].astype(BF16), ssmout_ref[...])

    merged = (_sigmoid(_dot(xb, wgp_ref[...])) * a_scr[...]
              + _sigmoid(_dot(xb, wgs_ref[...])) * m_out)
    mix = _dot(merged.astype(BF16), wout_ref[...])
    out_ref[0] = _layer_norm(DN_ALPHA * x + mix, ln1g_ref[...], ln1b_ref[...])


def _resident(shape):
    zeros = (0,) * len(shape)
    return pl.BlockSpec(shape, lambda b, s: zeros, pipeline_mode=pl.Buffered(1))


def _mixer(x, w_in, pool_w, pool_scale, conv_w, conv_b, dt_bias, a_log, d_skip, ssm_norm_w,
           ssm_out, w_out, ln1_g, ln1_b):
    bsz, seq, _ = x.shape
    tq = min(MIXER_TOKENS, seq)
    assert seq % tq == 0 and tq % SSM_CHUNK == 0
    o1 = D_MODEL
    o2 = o1 + SSM_INNER
    o3 = o2 + SSM_CONV_DIM
    o4 = o3 + SSM_HEADS
    o5 = o4 + D_MODEL
    pad_h = LANES - SSM_HEADS
    w_pool = w_in[:, :o1].astype(BF16)
    w_z = w_in[:, o1:o2].astype(BF16)
    w_xbc = w_in[:, o2:o3].astype(BF16)
    w_dt = jnp.pad(w_in[:, o3:o4], ((0, 0), (0, pad_h))).astype(BF16)
    w_gp = w_in[:, o4:o5].astype(BF16)
    w_gs = w_in[:, o5:].astype(BF16)
    row = lambda v: v.reshape(1, -1).astype(F32)
    dtb = jnp.pad(row(dt_bias), ((0, 0), (0, pad_h)))
    alog = jnp.pad(row(a_log), ((0, 0), (0, pad_h)))
    dskip = jnp.repeat(row(d_skip), SSM_HEAD_DIM, axis=1)

    q = SSM_CHUNK
    ltri = jnp.asarray(np.tril(np.ones((q, q), np.float32)), BF16)
    eh = np.zeros((LANES, SSM_INNER), np.float32)
    ec = np.zeros((LANES, SSM_HEADS * q), np.float32)
    for h in range(SSM_HEADS):
        eh[h, h * SSM_HEAD_DIM:(h + 1) * SSM_HEAD_DIM] = 1.0
        ec[h, h * q:(h + 1) * q] = 1.0
    ehead = jnp.asarray(eh, BF16)
    ecol = jnp.asarray(ec, BF16)

    operands = [
        x, w_pool, w_z, w_xbc, w_dt, w_gp, w_gs,
        pool_w.astype(BF16), row(pool_scale), conv_w.astype(F32), row(conv_b), dtb, alog, dskip,
        row(ssm_norm_w), ssm_out.astype(BF16), w_out.astype(BF16), row(ln1_g), row(ln1_b),
        ltri, ehead, ecol,
    ]
    in_specs = [pl.BlockSpec((1, tq, D_MODEL), lambda b, s: (b, s, 0))]
    in_specs += [_resident(op.shape) for op in operands[1:]]
    return pl.pallas_call(
        _mixer_kernel,
        grid=(bsz, seq // tq),
        in_specs=in_specs,
        out_specs=pl.BlockSpec((1, tq, D_MODEL), lambda b, s: (b, s, 0)),
        out_shape=jax.ShapeDtypeStruct(x.shape, F32),
        scratch_shapes=[
            pltpu.VMEM((POOL_HISTORY + tq, D_MODEL), F32),
            pltpu.VMEM((CONV_HISTORY + tq, SSM_CONV_DIM), F32),
            pltpu.VMEM((tq, SSM_CONV_DIM), F32),
            pltpu.VMEM((tq, SSM_INNER), F32),
            pltpu.VMEM((tq, D_MODEL), F32),
            pltpu.VMEM((SSM_GROUPS, SSM_STATE, SSM_GROUP_COLS), F32),
        ],
        compiler_params=pltpu.CompilerParams(
            dimension_semantics=("arbitrary", "arbitrary"),
            vmem_limit_bytes=VMEM_LIMIT_BYTES),
        name="mixer",
    )(*operands)


def _top16_rounds(scores, key_iota):
    cur = scores
    rank = jnp.full(scores.shape, float(PEER_TOPK), F32)
    tops = []
    for r in range(PEER_TOPK):
        m = jnp.max(cur, axis=0, keepdims=True)
        first = jnp.min(jnp.where(cur == m, key_iota, float(PEER_KEYS)), axis=0, keepdims=True)
        hit = key_iota == first
        cur = jnp.where(hit, -jnp.inf, cur)
        rank = jnp.where(hit, float(r), rank)
        tops.append(m)
    return rank, tops


def _route_kernel(h_ref, wqt_ref, keys_ref,
                  xt_ref, cnt_ref, e1_ref, rank2_ref, e2_ref,
                  qt_scr, rank1_scr, e1_scr, tops_scr, csel_scr):
    tr = h_ref.shape[0]
    xt = h_ref[...].T.astype(BF16)
    xt_ref[...] = xt
    qt_scr[...] = _dot(wqt_ref[...], xt)

    key_iota = lax.broadcasted_iota(jnp.int32, (PEER_KEYS, LANES), 0).astype(F32)
    head_iota = lax.broadcasted_iota(jnp.int32, (PEER_HEADS, LANES), 0)
    tops_scr[...] = jnp.zeros(tops_scr.shape, F32)

    def head_body(h, carry):
        for k in range(2):
            q0 = pl.multiple_of((h * 2 + k) * PEER_HALF, PEER_HALF)
            qk = qt_scr[pl.ds(q0, PEER_HALF), :].astype(BF16)
            scores = _dot(keys_ref[k], qk)
            for lt in range(tr // LANES):
                lanes = slice(lt * LANES, (lt + 1) * LANES)
                sc = scores[:, lanes]
                rank, tops = _top16_rounds(sc, key_iota)
                e = jnp.exp(sc - tops[0])
                if k == 0:
                    rank1_scr[h, :, lanes] = rank
                    e1_scr[h, :, lanes] = e
                else:
                    rank2_ref[h, :, lanes] = rank.astype(BF16)
                    e2_ref[h, :, lanes] = e.astype(BF16)
                for r in range(PEER_TOPK):
                    old = tops_scr[k, r, :, lanes]
                    tops_scr[k, r, :, lanes] = jnp.where(head_iota == h, tops[r], old)
        return carry

    lax.fori_loop(0, PEER_HEADS, head_body, 0)

    for lt in range(tr // LANES):
        lanes = slice(lt * LANES, (lt + 1) * LANES)
        s1 = [tops_scr[0, a, :, lanes] for a in range(PEER_TOPK)]
        s2 = [tops_scr[1, b, :, lanes] for b in range(PEER_TOPK)]
        sums = {ab: s1[ab[0]] + s2[ab[1]] for ab in _CANDIDATES}
        p1 = [jnp.exp(s1[a] - s1[0]) for a in range(PEER_TOPK)]
        p2 = [jnp.exp(s2[b] - s2[0]) for b in range(PEER_TOPK)]
        counts = [jnp.zeros((PEER_HEADS, LANES), F32) for _ in range(PEER_TOPK)]
        z = jnp.zeros((PEER_HEADS, LANES), F32)
        for (a, b) in _CANDIDATES:
            s = sums[(a, b)]
            beaten_by = jnp.full((PEER_HEADS, LANES), float((a + 1) * (b + 1) - 1), F32)
            for (a2, b2) in _CANDIDATES:
                if a2 < a and b2 > b:
                    beaten_by = beaten_by + jnp.where(sums[(a2, b2)] >= s, 1.0, 0.0)
                elif a2 > a and b2 < b:
                    beaten_by = beaten_by + jnp.where(sums[(a2, b2)] > s, 1.0, 0.0)
            sel = beaten_by < float(PEER_TOPK)
            counts[a] = counts[a] + jnp.where(sel, 1.0, 0.0)
            z = z + jnp.where(sel, p1[a] * p2[b], 0.0)
        for a in range(PEER_TOPK):
            csel_scr[a, :, lanes] = counts[a]
        csel_scr[PEER_TOPK, :, lanes] = 1.0 / z

    for h in range(PEER_HEADS):
        for lt in range(tr // LANES):
            lanes = slice(lt * LANES, (lt + 1) * LANES)
            rank = rank1_scr[h, :, lanes]
            cnt = jnp.zeros((PEER_KEYS, LANES), F32)
            for a in range(PEER_TOPK):
                cnt = jnp.where(rank == float(a), csel_scr[a, h:h + 1, lanes], cnt)
            cnt_ref[h, :, lanes] = cnt
            e1_ref[h, :, lanes] = e1_scr[h, :, lanes] * csel_scr[PEER_TOPK, h:h + 1, lanes]


def _route(h1, w_q, sub_keys):
    t = h1.shape[0]
    tr = min(ROUTE_TOKENS, t)
    assert t % tr == 0 and tr % LANES == 0
    wqt = w_q.T.astype(BF16)
    keys = sub_keys.astype(BF16)
    nq = wqt.shape[0]
    table = lambda dt: jax.ShapeDtypeStruct((PEER_HEADS, PEER_KEYS, t), dt)
    table_spec = pl.BlockSpec((PEER_HEADS, PEER_KEYS, tr), lambda i: (0, 0, i))
    return pl.pallas_call(
        _route_kernel,
        grid=(t // tr,),
        in_specs=[
            pl.BlockSpec((tr, D_MODEL), lambda i: (i, 0)),
            pl.BlockSpec(wqt.shape, lambda i: (0, 0), pipeline_mode=pl.Buffered(1)),
            pl.BlockSpec(keys.shape, lambda i: (0, 0, 0), pipeline_mode=pl.Buffered(1)),
        ],
        out_specs=[
            pl.BlockSpec((D_MODEL, tr), lambda i: (0, i)),
            table_spec, table_spec, table_spec, table_spec,
        ],
        out_shape=[
            jax.ShapeDtypeStruct((D_MODEL, t), BF16),
            table(F32), table(F32), table(BF16), table(BF16),
        ],
        scratch_shapes=[
            pltpu.VMEM((nq, tr), F32),
            pltpu.VMEM((PEER_HEADS, PEER_KEYS, tr), F32),
            pltpu.VMEM((PEER_HEADS, PEER_KEYS, tr), F32),
            pltpu.VMEM((2, PEER_TOPK, PEER_HEADS, tr), F32),
            pltpu.VMEM((PEER_TOPK + 1, PEER_HEADS, tr), F32),
        ],
        compiler_params=pltpu.CompilerParams(
            dimension_semantics=("arbitrary",),
            vmem_limit_bytes=VMEM_LIMIT_BYTES),
        name="route",
    )(h1, wqt, keys)


def _expert_kernel(xt_ref, cnt_ref, e1_ref, rank2_ref, e2_ref, u_ref, vt_ref,
                   h1_ref, ln2g_ref, ln2b_ref,
                   out_ref,
                   acc_scr, act_scr, pre_a, pre_b):
    e_idx = pl.program_id(1)
    n_blocks = pl.num_programs(1) - 1
    sqrt_half = math.sqrt(0.5)

    def pre_activations(dst):
        dst[...] = _dot(u_ref[...], xt_ref[...]).astype(BF16)

    def finish_block(src):
        for i in range(EXPERT_ROWS):
            rows = slice(i * PEER_KEYS, (i + 1) * PEER_KEYS)
            hi = src[rows, :]
            gelu = 0.5 * hi * (1.0 + lax.erf(hi * sqrt_half))
            w = None
            for h in range(PEER_HEADS):
                cnt = cnt_ref[h, i:i + 1, :].astype(BF16)
                e1 = e1_ref[h, i:i + 1, :].astype(BF16)
                term = jnp.where(rank2_ref[h] < cnt, e2_ref[h], jnp.zeros((), BF16)) * e1
                w = term if w is None else w + term
            act_scr[rows, :] = gelu * w
        return _dot(vt_ref[...], act_scr[...])

    @pl.when(e_idx == 0)
    def _():
        acc_scr[...] = jnp.zeros(acc_scr.shape, F32)
        pre_activations(pre_a)

    for parity, (dst, src) in enumerate(((pre_a, pre_b), (pre_b, pre_a))):
        @pl.when((e_idx > 0) & (e_idx < n_blocks) & (e_idx % 2 == parity))
        def _(dst=dst, src=src):
            pre_activations(dst)
            acc_scr[...] += finish_block(src)

    @pl.when(e_idx == n_blocks)
    def _():
        total = acc_scr[...] + finish_block(pre_a if EXPERT_LAST_IN_A else pre_b)
        res = DN_ALPHA * h1_ref[...] + total.T
        out_ref[...] = _layer_norm(res, ln2g_ref[...], ln2b_ref[...])


def _experts(h1, xt, cnt, e1, rank2, e2, expert_u, expert_v, ln2_g, ln2_b):
    t = h1.shape[0]
    tm = min(EXPERT_TOKENS, t)
    assert t % tm == 0
    n_exp = expert_u.shape[0]
    eb = EXPERT_ROWS * PEER_KEYS
    assert n_exp == EXPERT_BLOCKS * eb
    u = expert_u.astype(BF16)
    vt = expert_v.T.astype(BF16)
    last = EXPERT_BLOCKS - 1
    prev = lambda e: jnp.maximum(e - 1, 0)
    tab_full = pl.BlockSpec((PEER_HEADS, PEER_KEYS, tm), lambda i, e: (0, 0, i))
    tab_rows = pl.BlockSpec((PEER_HEADS, EXPERT_ROWS, tm), lambda i, e: (0, prev(e), i))
    vec = lambda v: v.reshape(1, -1).astype(F32)
    return pl.pallas_call(
        _expert_kernel,
        grid=(t // tm, EXPERT_BLOCKS + 1),
        in_specs=[
            pl.BlockSpec((D_MODEL, tm), lambda i, e: (0, i)),
            tab_rows, tab_rows, tab_full, tab_full,
            pl.BlockSpec((eb, D_MODEL), lambda i, e: (jnp.minimum(e, last), 0)),
            pl.BlockSpec((D_MODEL, eb), lambda i, e: (0, prev(e))),
            pl.BlockSpec((tm, D_MODEL), lambda i, e: (i, 0)),
            pl.BlockSpec((1, D_MODEL), lambda i, e: (0, 0)),
            pl.BlockSpec((1, D_MODEL), lambda i, e: (0, 0)),
        ],
        out_specs=pl.BlockSpec((tm, D_MODEL), lambda i, e: (i, 0)),
        out_shape=jax.ShapeDtypeStruct((t, D_MODEL), F32),
        scratch_shapes=[
            pltpu.VMEM((D_MODEL, tm), F32),
            pltpu.VMEM((eb, tm), BF16),
            pltpu.VMEM((eb, tm), BF16),
            pltpu.VMEM((eb, tm), BF16),
        ],
        compiler_params=pltpu.CompilerParams(
            dimension_semantics=("arbitrary", "arbitrary"),
            vmem_limit_bytes=VMEM_LIMIT_BYTES),
        name="experts",
    )(xt, cnt, e1, rank2, e2, u, vt, h1, vec(ln2_g), vec(ln2_b))


def kernel(x, w_in, pool_w, pool_scale, conv_w, conv_b, dt_bias, a_log, d_skip, ssm_norm_w,
           ssm_out, w_out, ln1_g, ln1_b, w_q, sub_keys, expert_u, expert_v, ln2_g, ln2_b):
    bsz, seq, d = x.shape
    h = x
    for i in range(DEPTH):
        h1 = _mixer(h, w_in[i], pool_w[i], pool_scale[i], conv_w[i], conv_b[i], dt_bias[i],
                    a_log[i], d_skip[i], ssm_norm_w[i], ssm_out[i], w_out[i], ln1_g[i], ln1_b[i])
        h1 = h1.reshape(bsz * seq, d)
        xt, cnt, e1, rank2, e2 = _route(h1, w_q[i], sub_keys[i])
        h = _experts(h1, xt, cnt, e1, rank2, e2, expert_u[i], expert_v[i], ln2_g[i], ln2_b[i])
        h = h.reshape(bsz, seq, d)
    return h
```

```python
import functools
import math

import numpy as np
import jax
import jax.numpy as jnp
from jax import lax
from jax.experimental import pallas as pl
from jax.experimental.pallas import tpu as pltpu

F32 = jnp.float32
BF16 = jnp.bfloat16

D_MODEL = 1024
POOL_WINDOWS = (2, 4, 8, 16)
POOL_GROUPS = len(POOL_WINDOWS)
POOL_GROUP_DIM = D_MODEL // POOL_GROUPS
POOL_HISTORY = 16
SSM_INNER = 2 * D_MODEL
SSM_HEAD_DIM = 64
SSM_HEADS = SSM_INNER // SSM_HEAD_DIM
SSM_GROUPS = 8
SSM_HEADS_PER_GROUP = SSM_HEADS // SSM_GROUPS
SSM_STATE = 128
SSM_CONV = 4
SSM_CHUNK = 128
SSM_BC = SSM_GROUPS * SSM_STATE
SSM_CONV_DIM = SSM_INNER + 2 * SSM_BC
SSM_NORM_GROUP = SSM_INNER // SSM_GROUPS
SSM_GROUP_COLS = SSM_HEADS_PER_GROUP * SSM_HEAD_DIM
CONV_HISTORY = 8
PEER_HEADS = 8
PEER_KEYS = 128
PEER_HALF = 128
PEER_TOPK = 16
DEPTH = 1
DN_ALPHA = (2.0 * DEPTH) ** 0.25
LN_EPS = 1e-5
RMS_EPS = 1e-5

LANES = 128
BF16_SUBLANES = 16
VMEM_LIMIT_BYTES = 56 * 1024 * 1024

MIXER_TOKENS = 256
XBC_COL_BLOCK = 512
ROUTE_TOKENS = 256
EXPERT_TOKENS = 512
EXPERT_ROWS = 16
EXPERT_BLOCKS = PEER_KEYS // EXPERT_ROWS
EXPERT_LAST_IN_A = (EXPERT_BLOCKS - 1) % 2 == 0

_CANDIDATES = [(a, b) for a in range(PEER_TOPK) for b in range(PEER_TOPK)
               if (a + 1) * (b + 1) <= PEER_TOPK]


def _dot(a, b):
    return jnp.dot(a, b, preferred_element_type=F32)


def _split_dot_left(onehot, v, terms):
    acc = None
    rem = v
    for _ in range(terms):
        part = rem.astype(BF16)
        d = _dot(onehot, part)
        acc = d if acc is None else acc + d
        rem = rem - part.astype(F32)
    return acc


def _sigmoid(x):
    return 1.0 / (1.0 + jnp.exp(-x))


def _silu(x):
    return x * _sigmoid(x)


def _layer_norm(x, g, b):
    mu = jnp.mean(x, axis=-1, keepdims=True)
    xc = x - mu
    var = jnp.mean(xc * xc, axis=-1, keepdims=True)
    return xc * lax.rsqrt(var + LN_EPS) * g + b


def _mixer_kernel(x_ref, wpool_ref, wz_ref, wxbc_ref, wdt_ref, wgp_ref, wgs_ref,
                  poolw_ref, pscale_ref, convw_ref, convb_ref, dtb_ref, alog_ref, dskip_ref,
                  normw_ref, ssmout_ref, wout_ref, ln1g_ref, ln1b_ref,
                  ltri_ref, ehead_ref,
                  out_ref,
                  pool_ext, xbc_ext, xbc_act, y_scr, a_scr, state):
    tq = x_ref.shape[1]
    s_idx = pl.program_id(1)

    @pl.when(s_idx == 0)
    def _():
        pool_ext[0:POOL_HISTORY, :] = jnp.zeros((POOL_HISTORY, D_MODEL), F32)
        xbc_ext[0:CONV_HISTORY, :] = jnp.zeros((CONV_HISTORY, SSM_CONV_DIM), F32)
        state[...] = jnp.zeros(state.shape, F32)

    x = x_ref[0]
    xb = x.astype(BF16)

    pool_ext[POOL_HISTORY:POOL_HISTORY + tq, :] = _dot(xb, wpool_ref[...])
    pos = (s_idx * tq + lax.broadcasted_iota(jnp.int32, (tq, 1), 0) + 1).astype(F32)
    for g, w in enumerate(POOL_WINDOWS):
        cols = slice(g * POOL_GROUP_DIM, (g + 1) * POOL_GROUP_DIM)
        u = pool_ext[POOL_HISTORY:POOL_HISTORY + tq, cols]
        acc = u
        for k in range(1, w):
            acc = acc + pool_ext[POOL_HISTORY - k:POOL_HISTORY - k + tq, cols]
        pooled = acc / jnp.minimum(pos, float(w)) - u
        mixed = _dot(pooled.astype(BF16), poolw_ref[g])
        a_scr[:, cols] = mixed * pscale_ref[:, cols]
    pool_ext[0:POOL_HISTORY, :] = pool_ext[tq:tq + POOL_HISTORY, :]

    for cb in range(SSM_CONV_DIM // XBC_COL_BLOCK):
        cols = slice(cb * XBC_COL_BLOCK, (cb + 1) * XBC_COL_BLOCK)
        xbc_ext[CONV_HISTORY:CONV_HISTORY + tq, cols] = _dot(xb, wxbc_ref[:, cols])
        conv = convb_ref[:, cols]
        for k in range(SSM_CONV):
            r0 = CONV_HISTORY - (SSM_CONV - 1) + k
            conv = conv + convw_ref[k:k + 1, cols] * xbc_ext[r0:r0 + tq, cols]
        xbc_act[:, cols] = _silu(conv)
    xbc_ext[0:CONV_HISTORY, :] = xbc_ext[tq:tq + CONV_HISTORY, :]

    dtr = _dot(xb, wdt_ref[...]) + dtb_ref[...]
    dt = jnp.maximum(dtr, 0.0) + jnp.log(1.0 + jnp.exp(-jnp.abs(dtr)))
    adt = dt * (-jnp.exp(alog_ref[...]))

    tri = (lax.broadcasted_iota(jnp.int32, (SSM_CHUNK, SSM_CHUNK), 0)
           >= lax.broadcasted_iota(jnp.int32, (SSM_CHUNK, SSM_CHUNK), 1))
    ltri = ltri_ref[...]
    ehead = ehead_ref[...]
    head_of_row = lax.broadcasted_iota(jnp.int32, (SSM_HEADS_PER_GROUP * SSM_CHUNK, SSM_GROUP_COLS), 0) // SSM_CHUNK
    head_of_col = lax.broadcasted_iota(jnp.int32, (SSM_HEADS_PER_GROUP * SSM_CHUNK, SSM_GROUP_COLS), 1) // SSM_HEAD_DIM
    own_head = head_of_row == head_of_col
    x_off, b_off, c_off = 0, SSM_INNER, SSM_INNER + SSM_BC

    for c in range(tq // SSM_CHUNK):
        rows = slice(c * SSM_CHUNK, (c + 1) * SSM_CHUNK)
        acum = _split_dot_left(ltri, adt[rows], 3)
        acum_t = acum.T
        alast = acum[SSM_CHUNK - 1:SSM_CHUNK, :]
        ea = jnp.exp(acum)
        ea_hi = ea.astype(BF16)
        ea_lo = (ea - ea_hi.astype(F32)).astype(BF16)
        stack = jnp.concatenate([dt[rows].astype(BF16), jnp.exp(alast - acum).astype(BF16),
                                 ea_hi, ea_lo], axis=0)
        expanded = _dot(stack, ehead)
        dt_e = expanded[0:SSM_CHUNK]
        dec_e = expanded[SSM_CHUNK:2 * SSM_CHUNK]
        ea_e = expanded[2 * SSM_CHUNK:3 * SSM_CHUNK] + expanded[3 * SSM_CHUNK:4 * SSM_CHUNK]

        for g in range(SSM_GROUPS):
            gcols = slice(g * SSM_GROUP_COLS, (g + 1) * SSM_GROUP_COLS)
            ncols_b = slice(b_off + g * SSM_STATE, b_off + (g + 1) * SSM_STATE)
            ncols_c = slice(c_off + g * SSM_STATE, c_off + (g + 1) * SSM_STATE)
            cg = xbc_act[rows, ncols_c].astype(BF16)
            bg_t = xbc_act[rows, ncols_b].T.astype(BF16)
            cb_mat = _dot(cg, bg_t)
            xs_g = xbc_act[rows, x_off + g * SSM_GROUP_COLS:x_off + (g + 1) * SSM_GROUP_COLS]
            xd_g = xs_g * dt_e[:, gcols]
            g_parts = []
            for r in range(SSM_HEADS_PER_GROUP):
                hd = g * SSM_HEADS_PER_GROUP + r
                diff = acum[:, hd:hd + 1] - acum_t[hd:hd + 1, :]
                lmat = jnp.where(tri, jnp.exp(diff), 0.0)
                g_parts.append((cb_mat * lmat).astype(BF16))
            xd_b = xd_g.astype(BF16)
            xd_stack = jnp.where(own_head, jnp.concatenate([xd_b] * SSM_HEADS_PER_GROUP, axis=0),
                                 jnp.zeros((), BF16))
            y_diag = _dot(jnp.concatenate(g_parts, axis=1), xd_stack)
            st = state[g]
            y_off = _dot(cg, st.astype(BF16)) * ea_e[:, gcols]
            y_scr[rows, gcols] = y_diag + y_off + dskip_ref[:, gcols] * xs_g
            upd = _dot(bg_t, (xd_g * dec_e[:, gcols]).astype(BF16))
            state[g] = st * ea_e[SSM_CHUNK - 1:SSM_CHUNK, gcols] + upd

    y = y_scr[...] * _silu(_dot(xb, wz_ref[...]))
    for g in range(SSM_GROUPS):
        ncols = slice(g * SSM_NORM_GROUP, (g + 1) * SSM_NORM_GROUP)
        yg = y[:, ncols]
        ms = jnp.mean(yg * yg, axis=-1, keepdims=True)
        y_scr[:, ncols] = yg * lax.rsqrt(ms + RMS_EPS) * normw_ref[:, ncols]
    m_out = _dot(y_scr[...].astype(BF16), ssmout_ref[...])

    merged = (_sigmoid(_dot(xb, wgp_ref[...])) * a_scr[...]
              + _sigmoid(_dot(xb, wgs_ref[...])) * m_out)
    mix = _dot(merged.astype(BF16), wout_ref[...])
    out_ref[0] = _layer_norm(DN_ALPHA * x + mix, ln1g_ref[...], ln1b_ref[...])


def _resident(shape):
    zeros = (0,) * len(shape)
    return pl.BlockSpec(shape, lambda b, s: zeros, pipeline_mode=pl.Buffered(1))


def _mixer(x, w_in, pool_w, pool_scale, conv_w, conv_b, dt_bias, a_log, d_skip, ssm_norm_w,
           ssm_out, w_out, ln1_g, ln1_b):
    bsz, seq, _ = x.shape
    tq = min(MIXER_TOKENS, seq)
    assert seq % tq == 0 and tq % SSM_CHUNK == 0
    o1 = D_MODEL
    o2 = o1 + SSM_INNER
    o3 = o2 + SSM_CONV_DIM
    o4 = o3 + SSM_HEADS
    o5 = o4 + D_MODEL
    pad_h = LANES - SSM_HEADS
    w_pool = w_in[:, :o1].astype(BF16)
    w_z = w_in[:, o1:o2].astype(BF16)
    w_xbc = w_in[:, o2:o3].astype(BF16)
    w_dt = jnp.pad(w_in[:, o3:o4], ((0, 0), (0, pad_h))).astype(BF16)
    w_gp = w_in[:, o4:o5].astype(BF16)
    w_gs = w_in[:, o5:].astype(BF16)
    row = lambda v: v.reshape(1, -1).astype(F32)
    dtb = jnp.pad(row(dt_bias), ((0, 0), (0, pad_h)))
    alog = jnp.pad(row(a_log), ((0, 0), (0, pad_h)))
    dskip = jnp.repeat(row(d_skip), SSM_HEAD_DIM, axis=1)

    q = SSM_CHUNK
    ltri = jnp.asarray(np.tril(np.ones((q, q), np.float32)), BF16)
    eh = np.zeros((LANES, SSM_INNER), np.float32)
    for h in range(SSM_HEADS):
        eh[h, h * SSM_HEAD_DIM:(h + 1) * SSM_HEAD_DIM] = 1.0
    ehead = jnp.asarray(eh, BF16)

    operands = [
        x, w_pool, w_z, w_xbc, w_dt, w_gp, w_gs,
        pool_w.astype(BF16), row(pool_scale), conv_w.astype(F32), row(conv_b), dtb, alog, dskip,
        row(ssm_norm_w), ssm_out.astype(BF16), w_out.astype(BF16), row(ln1_g), row(ln1_b),
        ltri, ehead,
    ]
    in_specs = [pl.BlockSpec((1, tq, D_MODEL), lambda b, s: (b, s, 0))]
    in_specs += [_resident(op.shape) for op in operands[1:]]
    return pl.pallas_call(
        _mixer_kernel,
        grid=(bsz, seq // tq),
        in_specs=in_specs,
        out_specs=pl.BlockSpec((1, tq, D_MODEL), lambda b, s: (b, s, 0)),
        out_shape=jax.ShapeDtypeStruct(x.shape, F32),
        scratch_shapes=[
            pltpu.VMEM((POOL_HISTORY + tq, D_MODEL), F32),
            pltpu.VMEM((CONV_HISTORY + tq, SSM_CONV_DIM), F32),
            pltpu.VMEM((tq, SSM_CONV_DIM), F32),
            pltpu.VMEM((tq, SSM_INNER), F32),
            pltpu.VMEM((tq, D_MODEL), F32),
            pltpu.VMEM((SSM_GROUPS, SSM_STATE, SSM_GROUP_COLS), F32),
        ],
        compiler_params=pltpu.CompilerParams(
            dimension_semantics=("arbitrary", "arbitrary"),
            vmem_limit_bytes=VMEM_LIMIT_BYTES),
        name="mixer",
    )(*operands)


def _top16_rounds(scores, key_iota, break_ties):
    cur = scores
    rank = jnp.full(scores.shape, float(PEER_TOPK), F32)
    tops = []
    for r in range(PEER_TOPK):
        m = jnp.max(cur, axis=0, keepdims=True)
        hit = cur == m
        if break_ties:
            first = jnp.min(jnp.where(hit, key_iota, float(PEER_KEYS)), axis=0, keepdims=True)
            hit = key_iota == first
        cur = jnp.where(hit, -jnp.inf, cur)
        rank = jnp.where(hit, float(r), rank)
        tops.append(m)
    n_sel = jnp.sum(jnp.where(rank < float(PEER_TOPK), 1.0, 0.0), axis=0, keepdims=True)
    return rank, tops, n_sel


def _route_kernel(h_ref, wqt_ref, keys_ref,
                  xt_ref, cnt_ref, e1_ref, rank2_ref, e2_ref,
                  qt_scr, rank1_scr, e1_scr, tops_scr, csel_scr):
    tr = h_ref.shape[0]
    xt = h_ref[...].T.astype(BF16)
    xt_ref[...] = xt
    qt_scr[...] = _dot(wqt_ref[...], xt)

    key_iota = lax.broadcasted_iota(jnp.int32, (PEER_KEYS, LANES), 0).astype(F32)
    head_iota = lax.broadcasted_iota(jnp.int32, (PEER_HEADS, LANES), 0)
    tops_scr[...] = jnp.zeros(tops_scr.shape, F32)

    def stage1(break_ties):
        def head_body(h, miscount):
            for k in range(2):
                q0 = pl.multiple_of((h * 2 + k) * PEER_HALF, PEER_HALF)
                qk = qt_scr[pl.ds(q0, PEER_HALF), :].astype(BF16)
                scores = _dot(keys_ref[k], qk)
                for lt in range(tr // LANES):
                    lanes = slice(lt * LANES, (lt + 1) * LANES)
                    sc = scores[:, lanes]
                    rank, tops, n_sel = _top16_rounds(sc, key_iota, break_ties)
                    miscount = jnp.maximum(miscount, jnp.abs(n_sel - float(PEER_TOPK)))
                    e = jnp.exp(sc - tops[0])
                    if k == 0:
                        rank1_scr[h, :, lanes] = rank
                        e1_scr[h, :, lanes] = e
                    else:
                        rank2_ref[h, :, lanes] = rank.astype(BF16)
                        e2_ref[h, :, lanes] = e.astype(BF16)
                    for r in range(PEER_TOPK):
                        old = tops_scr[k, r, :, lanes]
                        tops_scr[k, r, :, lanes] = jnp.where(head_iota == h, tops[r], old)
            return miscount

        return lax.fori_loop(0, PEER_HEADS, head_body, jnp.zeros((1, LANES), F32))

    miscount = stage1(break_ties=False)

    @pl.when(jnp.max(miscount) > 0.0)
    def _():
        stage1(break_ties=True)

    for lt in range(tr // LANES):
        lanes = slice(lt * LANES, (lt + 1) * LANES)
        s1 = [tops_scr[0, a, :, lanes] for a in range(PEER_TOPK)]
        s2 = [tops_scr[1, b, :, lanes] for b in range(PEER_TOPK)]
        sums = {ab: s1[ab[0]] + s2[ab[1]] for ab in _CANDIDATES}
        p1 = [jnp.exp(s1[a] - s1[0]) for a in range(PEER_TOPK)]
        p2 = [jnp.exp(s2[b] - s2[0]) for b in range(PEER_TOPK)]
        counts = [jnp.zeros((PEER_HEADS, LANES), F32) for _ in range(PEER_TOPK)]
        z = jnp.zeros((PEER_HEADS, LANES), F32)
        for (a, b) in _CANDIDATES:
            s = sums[(a, b)]
            beaten_by = jnp.full((PEER_HEADS, LANES), float((a + 1) * (b + 1) - 1), F32)
            for (a2, b2) in _CANDIDATES:
                if a2 < a and b2 > b:
                    beaten_by = beaten_by + jnp.where(sums[(a2, b2)] >= s, 1.0, 0.0)
                elif a2 > a and b2 < b:
                    beaten_by = beaten_by + jnp.where(sums[(a2, b2)] > s, 1.0, 0.0)
            sel = beaten_by < float(PEER_TOPK)
            counts[a] = counts[a] + jnp.where(sel, 1.0, 0.0)
            z = z + jnp.where(sel, p1[a] * p2[b], 0.0)
        for a in range(PEER_TOPK):
            csel_scr[a, :, lanes] = counts[a]
        csel_scr[PEER_TOPK, :, lanes] = 1.0 / z

    for h in range(PEER_HEADS):
        for lt in range(tr // LANES):
            lanes = slice(lt * LANES, (lt + 1) * LANES)
            rank = rank1_scr[h, :, lanes]
            cnt = jnp.zeros((PEER_KEYS, LANES), F32)
            for a in range(PEER_TOPK):
                cnt = jnp.where(rank == float(a), csel_scr[a, h:h + 1, lanes], cnt)
            cnt_ref[h, :, lanes] = cnt
            e1_ref[h, :, lanes] = e1_scr[h, :, lanes] * csel_scr[PEER_TOPK, h:h + 1, lanes]


def _route(h1, w_q, sub_keys):
    t = h1.shape[0]
    tr = min(ROUTE_TOKENS, t)
    assert t % tr == 0 and tr % LANES == 0
    wqt = w_q.T.astype(BF16)
    keys = sub_keys.astype(BF16)
    nq = wqt.shape[0]
    table = lambda dt: jax.ShapeDtypeStruct((PEER_HEADS, PEER_KEYS, t), dt)
    table_spec = pl.BlockSpec((PEER_HEADS, PEER_KEYS, tr), lambda i: (0, 0, i))
    return pl.pallas_call(
        _route_kernel,
        grid=(t // tr,),
        in_specs=[
            pl.BlockSpec((tr, D_MODEL), lambda i: (i, 0)),
            pl.BlockSpec(wqt.shape, lambda i: (0, 0), pipeline_mode=pl.Buffered(1)),
            pl.BlockSpec(keys.shape, lambda i: (0, 0, 0), pipeline_mode=pl.Buffered(1)),
        ],
        out_specs=[
            pl.BlockSpec((D_MODEL, tr), lambda i: (0, i)),
            table_spec, table_spec, table_spec, table_spec,
        ],
        out_shape=[
            jax.ShapeDtypeStruct((D_MODEL, t), BF16),
            table(F32), table(F32), table(BF16), table(BF16),
        ],
        scratch_shapes=[
            pltpu.VMEM((nq, tr), F32),
            pltpu.VMEM((PEER_HEADS, PEER_KEYS, tr), F32),
            pltpu.VMEM((PEER_HEADS, PEER_KEYS, tr), F32),
            pltpu.VMEM((2, PEER_TOPK, PEER_HEADS, tr), F32),
            pltpu.VMEM((PEER_TOPK + 1, PEER_HEADS, tr), F32),
        ],
        compiler_params=pltpu.CompilerParams(
            dimension_semantics=("arbitrary",),
            vmem_limit_bytes=VMEM_LIMIT_BYTES),
        name="route",
    )(h1, wqt, keys)


def _expert_kernel(xt_ref, cnt_ref, e1_ref, rank2_ref, e2_ref, u_ref, vt_ref,
                   h1_ref, ln2g_ref, ln2b_ref,
                   out_ref,
                   acc_scr, act_scr, pre_a, pre_b):
    e_idx = pl.program_id(1)
    n_blocks = pl.num_programs(1) - 1
    sqrt_half = math.sqrt(0.5)

    def pre_activations(dst):
        dst[...] = _dot(u_ref[...], xt_ref[...]).astype(BF16)

    def row_to_keys(row):
        tile = jnp.broadcast_to(row, (BF16_SUBLANES, row.shape[1])).astype(BF16)
        reps = PEER_KEYS // BF16_SUBLANES
        return jnp.broadcast_to(tile[None], (reps,) + tile.shape).reshape(PEER_KEYS, row.shape[1])

    def finish_block(src):
        for i in range(EXPERT_ROWS):
            rows = slice(i * PEER_KEYS, (i + 1) * PEER_KEYS)
            hi = src[rows, :]
            gelu = 0.5 * hi * (1.0 + lax.erf(hi * sqrt_half))
            w = None
            for h in range(PEER_HEADS):
                cnt = row_to_keys(cnt_ref[h, i:i + 1, :])
                e1 = row_to_keys(e1_ref[h, i:i + 1, :])
                term = jnp.where(rank2_ref[h] < cnt, e2_ref[h], jnp.zeros((), BF16)) * e1
                w = term if w is None else w + term
            act_scr[rows, :] = gelu * w
        return _dot(vt_ref[...], act_scr[...])

    @pl.when(e_idx == 0)
    def _():
        acc_scr[...] = jnp.zeros(acc_scr.shape, F32)
        pre_activations(pre_a)

    for parity, (dst, src) in enumerate(((pre_a, pre_b), (pre_b, pre_a))):
        @pl.when((e_idx > 0) & (e_idx < n_blocks) & (e_idx % 2 == parity))
        def _(dst=dst, src=src):
            pre_activations(dst)
            acc_scr[...] += finish_block(src)

    @pl.when(e_idx == n_blocks)
    def _():
        total = acc_scr[...] + finish_block(pre_a if EXPERT_LAST_IN_A else pre_b)
        res = DN_ALPHA * h1_ref[...] + total.T
        out_ref[...] = _layer_norm(res, ln2g_ref[...], ln2b_ref[...])


def _experts(h1, xt, cnt, e1, rank2, e2, expert_u, expert_v, ln2_g, ln2_b):
    t = h1.shape[0]
    tm = min(EXPERT_TOKENS, t)
    assert t % tm == 0
    n_exp = expert_u.shape[0]
    eb = EXPERT_ROWS * PEER_KEYS
    assert n_exp == EXPERT_BLOCKS * eb
    u = expert_u.astype(BF16)
    vt = expert_v.astype(BF16).reshape(EXPERT_BLOCKS, eb, D_MODEL).transpose(0, 2, 1)
    last = EXPERT_BLOCKS - 1
    prev = lambda e: jnp.maximum(e - 1, 0)
    tab_full = pl.BlockSpec((PEER_HEADS, PEER_KEYS, tm), lambda i, e: (0, 0, i))
    tab_rows = pl.BlockSpec((PEER_HEADS, EXPERT_ROWS, tm), lambda i, e: (0, prev(e), i))
    vec = lambda v: v.reshape(1, -1).astype(F32)
    return pl.pallas_call(
        _expert_kernel,
        grid=(t // tm, EXPERT_BLOCKS + 1),
        in_specs=[
            pl.BlockSpec((D_MODEL, tm), lambda i, e: (0, i)),
            tab_rows, tab_rows, tab_full, tab_full,
            pl.BlockSpec((eb, D_MODEL), lambda i, e: (jnp.minimum(e, last), 0)),
            pl.BlockSpec((None, D_MODEL, eb), lambda i, e: (prev(e), 0, 0)),
            pl.BlockSpec((tm, D_MODEL), lambda i, e: (i, 0), pipeline_mode=pl.Buffered(1)),
            pl.BlockSpec((1, D_MODEL), lambda i, e: (0, 0)),
            pl.BlockSpec((1, D_MODEL), lambda i, e: (0, 0)),
        ],
        out_specs=pl.BlockSpec((tm, D_MODEL), lambda i, e: (i, 0)),
        out_shape=jax.ShapeDtypeStruct((t, D_MODEL), F32),
        scratch_shapes=[
            pltpu.VMEM((D_MODEL, tm), F32),
            pltpu.VMEM((eb, tm), BF16),
            pltpu.VMEM((eb, tm), BF16),
            pltpu.VMEM((eb, tm), BF16),
        ],
        compiler_params=pltpu.CompilerParams(
            dimension_semantics=("arbitrary", "arbitrary"),
            vmem_limit_bytes=VMEM_LIMIT_BYTES),
        name="experts",
    )(xt, cnt, e1, rank2, e2, u, vt, h1, vec(ln2_g), vec(ln2_b))


def kernel(x, w_in, pool_w, pool_scale, conv_w, conv_b, dt_bias, a_log, d_skip, ssm_norm_w,
           ssm_out, w_out, ln1_g, ln1_b, w_q, sub_keys, expert_u, expert_v, ln2_g, ln2_b):
    bsz, seq, d = x.shape
    h = x
    for i in range(DEPTH):
        h1 = _mixer(h, w_in[i], pool_w[i], pool_scale[i], conv_w[i], conv_b[i], dt_bias[i],
                    a_log[i], d_skip[i], ssm_norm_w[i], ssm_out[i], w_out[i], ln1_g[i], ln1_b[i])
        h1 = h1.reshape(bsz * seq, d)
        xt, cnt, e1, rank2, e2 = _route(h1, w_q[i], sub_keys[i])
        h = _experts(h1, xt, cnt, e1, rank2, e2, expert_u[i], expert_v[i], ln2_g[i], ln2_b[i])
        h = h.reshape(bsz, seq, d)
    return h
```

```python
import functools
import math

import numpy as np
import jax
import jax.numpy as jnp
from jax import lax
from jax.experimental import pallas as pl
from jax.experimental.pallas import tpu as pltpu

F32 = jnp.float32
BF16 = jnp.bfloat16

D_MODEL = 1024
POOL_WINDOWS = (2, 4, 8, 16)
POOL_GROUPS = len(POOL_WINDOWS)
POOL_GROUP_DIM = D_MODEL // POOL_GROUPS
POOL_HISTORY = 16
SSM_INNER = 2 * D_MODEL
SSM_HEAD_DIM = 64
SSM_HEADS = SSM_INNER // SSM_HEAD_DIM
SSM_GROUPS = 8
SSM_HEADS_PER_GROUP = SSM_HEADS // SSM_GROUPS
SSM_STATE = 128
SSM_CONV = 4
SSM_CHUNK = 128
SSM_BC = SSM_GROUPS * SSM_STATE
SSM_CONV_DIM = SSM_INNER + 2 * SSM_BC
SSM_NORM_GROUP = SSM_INNER // SSM_GROUPS
SSM_GROUP_COLS = SSM_HEADS_PER_GROUP * SSM_HEAD_DIM
CONV_HISTORY = 8
PEER_HEADS = 8
PEER_KEYS = 128
PEER_HALF = 128
PEER_TOPK = 16
DEPTH = 1
DN_ALPHA = (2.0 * DEPTH) ** 0.25
LN_EPS = 1e-5
RMS_EPS = 1e-5

LANES = 128
BF16_SUBLANES = 16
VMEM_LIMIT_BYTES = 56 * 1024 * 1024

MIXER_TOKENS = 256
XBC_COL_BLOCK = 512
ROUTE_TOKENS = 256
EXPERT_TOKENS = 512
EXPERT_ROWS = 16
EXPERT_BLOCKS = PEER_KEYS // EXPERT_ROWS
EXPERT_LAST_IN_A = (EXPERT_BLOCKS - 1) % 2 == 0

_CANDIDATES = [(a, b) for a in range(PEER_TOPK) for b in range(PEER_TOPK)
               if (a + 1) * (b + 1) <= PEER_TOPK]


def _dot(a, b):
    return jnp.dot(a, b, preferred_element_type=F32)


def _split_dot_left(onehot, v, terms):
    acc = None
    rem = v
    for _ in range(terms):
        part = rem.astype(BF16)
        d = _dot(onehot, part)
        acc = d if acc is None else acc + d
        rem = rem - part.astype(F32)
    return acc


def _sigmoid(x):
    return 1.0 / (1.0 + jnp.exp(-x))


def _silu(x):
    return x * _sigmoid(x)


def _layer_norm(x, g, b):
    mu = jnp.mean(x, axis=-1, keepdims=True)
    xc = x - mu
    var = jnp.mean(xc * xc, axis=-1, keepdims=True)
    return xc * lax.rsqrt(var + LN_EPS) * g + b


def _mixer_kernel(x_ref, wpool_ref, wz_ref, wxbc_ref, wdt_ref, wgp_ref, wgs_ref,
                  poolw_ref, pscale_ref, convw_ref, convb_ref, dtb_ref, alog_ref, dskip_ref,
                  normw_ref, ssmout_ref, wout_ref, ln1g_ref, ln1b_ref,
                  ltri_ref, ehead_ref,
                  out_ref,
                  pool_ext, xbc_ext, xbc_act, y_scr, a_scr, state):
    tq = x_ref.shape[1]
    s_idx = pl.program_id(1)

    @pl.when(s_idx == 0)
    def _():
        pool_ext[0:POOL_HISTORY, :] = jnp.zeros((POOL_HISTORY, D_MODEL), F32)
        xbc_ext[0:CONV_HISTORY, :] = jnp.zeros((CONV_HISTORY, SSM_CONV_DIM), F32)
        state[...] = jnp.zeros(state.shape, F32)

    x = x_ref[0]
    xb = x.astype(BF16)

    pool_ext[POOL_HISTORY:POOL_HISTORY + tq, :] = _dot(xb, wpool_ref[...])
    pos = (s_idx * tq + lax.broadcasted_iota(jnp.int32, (tq, 1), 0) + 1).astype(F32)
    for g, w in enumerate(POOL_WINDOWS):
        cols = slice(g * POOL_GROUP_DIM, (g + 1) * POOL_GROUP_DIM)
        u = pool_ext[POOL_HISTORY:POOL_HISTORY + tq, cols]
        acc = u
        for k in range(1, w):
            acc = acc + pool_ext[POOL_HISTORY - k:POOL_HISTORY - k + tq, cols]
        pooled = acc / jnp.minimum(pos, float(w)) - u
        mixed = _dot(pooled.astype(BF16), poolw_ref[g])
        a_scr[:, cols] = mixed * pscale_ref[:, cols]
    pool_ext[0:POOL_HISTORY, :] = pool_ext[tq:tq + POOL_HISTORY, :]

    for cb in range(SSM_CONV_DIM // XBC_COL_BLOCK):
        cols = slice(cb * XBC_COL_BLOCK, (cb + 1) * XBC_COL_BLOCK)
        xbc_ext[CONV_HISTORY:CONV_HISTORY + tq, cols] = _dot(xb, wxbc_ref[:, cols])
        conv = convb_ref[:, cols]
        for k in range(SSM_CONV):
            r0 = CONV_HISTORY - (SSM_CONV - 1) + k
            conv = conv + convw_ref[k:k + 1, cols] * xbc_ext[r0:r0 + tq, cols]
        xbc_act[:, cols] = _silu(conv)
    xbc_ext[0:CONV_HISTORY, :] = xbc_ext[tq:tq + CONV_HISTORY, :]

    dtr = _dot(xb, wdt_ref[...]) + dtb_ref[...]
    dt = jnp.maximum(dtr, 0.0) + jnp.log(1.0 + jnp.exp(-jnp.abs(dtr)))
    adt = dt * (-jnp.exp(alog_ref[...]))

    tri = (lax.broadcasted_iota(jnp.int32, (SSM_CHUNK, SSM_CHUNK), 0)
           >= lax.broadcasted_iota(jnp.int32, (SSM_CHUNK, SSM_CHUNK), 1))
    ltri = ltri_ref[...]
    ehead = ehead_ref[...]
    head_of_row = lax.broadcasted_iota(jnp.int32, (SSM_HEADS_PER_GROUP * SSM_CHUNK, SSM_GROUP_COLS), 0) // SSM_CHUNK
    head_of_col = lax.broadcasted_iota(jnp.int32, (SSM_HEADS_PER_GROUP * SSM_CHUNK, SSM_GROUP_COLS), 1) // SSM_HEAD_DIM
    own_head = head_of_row == head_of_col
    x_off, b_off, c_off = 0, SSM_INNER, SSM_INNER + SSM_BC

    for c in range(tq // SSM_CHUNK):
        rows = slice(c * SSM_CHUNK, (c + 1) * SSM_CHUNK)
        acum = _split_dot_left(ltri, adt[rows], 3)
        acum_t = acum.T
        alast = acum[SSM_CHUNK - 1:SSM_CHUNK, :]
        ea = jnp.exp(acum)
        ea_hi = ea.astype(BF16)
        ea_lo = (ea - ea_hi.astype(F32)).astype(BF16)
        stack = jnp.concatenate([dt[rows].astype(BF16), jnp.exp(alast - acum).astype(BF16),
                                 ea_hi, ea_lo], axis=0)
        expanded = _dot(stack, ehead)
        dt_e = expanded[0:SSM_CHUNK]
        dec_e = expanded[SSM_CHUNK:2 * SSM_CHUNK]
        ea_e = expanded[2 * SSM_CHUNK:3 * SSM_CHUNK] + expanded[3 * SSM_CHUNK:4 * SSM_CHUNK]

        for g in range(SSM_GROUPS):
            gcols = slice(g * SSM_GROUP_COLS, (g + 1) * SSM_GROUP_COLS)
            ncols_b = slice(b_off + g * SSM_STATE, b_off + (g + 1) * SSM_STATE)
            ncols_c = slice(c_off + g * SSM_STATE, c_off + (g + 1) * SSM_STATE)
            cg = xbc_act[rows, ncols_c].astype(BF16)
            bg_t = xbc_act[rows, ncols_b].T.astype(BF16)
            cb_mat = _dot(cg, bg_t)
            xs_g = xbc_act[rows, x_off + g * SSM_GROUP_COLS:x_off + (g + 1) * SSM_GROUP_COLS]
            xd_g = xs_g * dt_e[:, gcols]
            g_parts = []
            for r in range(SSM_HEADS_PER_GROUP):
                hd = g * SSM_HEADS_PER_GROUP + r
                diff = acum[:, hd:hd + 1] - acum_t[hd:hd + 1, :]
                lmat = jnp.where(tri, jnp.exp(diff), 0.0)
                g_parts.append((cb_mat * lmat).astype(BF16))
            xd_b = xd_g.astype(BF16)
            xd_stack = jnp.where(own_head, jnp.concatenate([xd_b] * SSM_HEADS_PER_GROUP, axis=0),
                                 jnp.zeros((), BF16))
            y_diag = _dot(jnp.concatenate(g_parts, axis=1), xd_stack)
            st = state[g]
            y_off = _dot(cg, st.astype(BF16)) * ea_e[:, gcols]
            y_scr[rows, gcols] = y_diag + y_off + dskip_ref[:, gcols] * xs_g
            upd = _dot(bg_t, (xd_g * dec_e[:, gcols]).astype(BF16))
            state[g] = st * ea_e[SSM_CHUNK - 1:SSM_CHUNK, gcols] + upd

    y = y_scr[...] * _silu(_dot(xb, wz_ref[...]))
    for g in range(SSM_GROUPS):
        ncols = slice(g * SSM_NORM_GROUP, (g + 1) * SSM_NORM_GROUP)
        yg = y[:, ncols]
        ms = jnp.mean(yg * yg, axis=-1, keepdims=True)
        y_scr[:, ncols] = yg * lax.rsqrt(ms + RMS_EPS) * normw_ref[:, ncols]
    m_out = _dot(y_scr[...].astype(BF16), ssmout_ref[...])

    merged = (_sigmoid(_dot(xb, wgp_ref[...])) * a_scr[...]
              + _sigmoid(_dot(xb, wgs_ref[...])) * m_out)
    mix = _dot(merged.astype(BF16), wout_ref[...])
    out_ref[0] = _layer_norm(DN_ALPHA * x + mix, ln1g_ref[...], ln1b_ref[...])


def _resident(shape):
    zeros = (0,) * len(shape)
    return pl.BlockSpec(shape, lambda b, s: zeros, pipeline_mode=pl.Buffered(1))


def _mixer(x, w_in, pool_w, pool_scale, conv_w, conv_b, dt_bias, a_log, d_skip, ssm_norm_w,
           ssm_out, w_out, ln1_g, ln1_b):
    bsz, seq, _ = x.shape
    tq = min(MIXER_TOKENS, seq)
    assert seq % tq == 0 and tq % SSM_CHUNK == 0
    o1 = D_MODEL
    o2 = o1 + SSM_INNER
    o3 = o2 + SSM_CONV_DIM
    o4 = o3 + SSM_HEADS
    o5 = o4 + D_MODEL
    pad_h = LANES - SSM_HEADS
    w_pool = w_in[:, :o1].astype(BF16)
    w_z = w_in[:, o1:o2].astype(BF16)
    w_xbc = w_in[:, o2:o3].astype(BF16)
    w_dt = jnp.pad(w_in[:, o3:o4], ((0, 0), (0, pad_h))).astype(BF16)
    w_gp = w_in[:, o4:o5].astype(BF16)
    w_gs = w_in[:, o5:].astype(BF16)
    row = lambda v: v.reshape(1, -1).astype(F32)
    dtb = jnp.pad(row(dt_bias), ((0, 0), (0, pad_h)))
    alog = jnp.pad(row(a_log), ((0, 0), (0, pad_h)))
    dskip = jnp.repeat(row(d_skip), SSM_HEAD_DIM, axis=1)

    q = SSM_CHUNK
    ltri = jnp.asarray(np.tril(np.ones((q, q), np.float32)), BF16)
    eh = np.zeros((LANES, SSM_INNER), np.float32)
    for h in range(SSM_HEADS):
        eh[h, h * SSM_HEAD_DIM:(h + 1) * SSM_HEAD_DIM] = 1.0
    ehead = jnp.asarray(eh, BF16)

    operands = [
        x, w_pool, w_z, w_xbc, w_dt, w_gp, w_gs,
        pool_w.astype(BF16), row(pool_scale), conv_w.astype(F32), row(conv_b), dtb, alog, dskip,
        row(ssm_norm_w), ssm_out.astype(BF16), w_out.astype(BF16), row(ln1_g), row(ln1_b),
        ltri, ehead,
    ]
    in_specs = [pl.BlockSpec((1, tq, D_MODEL), lambda b, s: (b, s, 0))]
    in_specs += [_resident(op.shape) for op in operands[1:]]
    return pl.pallas_call(
        _mixer_kernel,
        grid=(bsz, seq // tq),
        in_specs=in_specs,
        out_specs=pl.BlockSpec((1, tq, D_MODEL), lambda b, s: (b, s, 0)),
        out_shape=jax.ShapeDtypeStruct(x.shape, F32),
        scratch_shapes=[
            pltpu.VMEM((POOL_HISTORY + tq, D_MODEL), F32),
            pltpu.VMEM((CONV_HISTORY + tq, SSM_CONV_DIM), F32),
            pltpu.VMEM((tq, SSM_CONV_DIM), F32),
            pltpu.VMEM((tq, SSM_INNER), F32),
            pltpu.VMEM((tq, D_MODEL), F32),
            pltpu.VMEM((SSM_GROUPS, SSM_STATE, SSM_GROUP_COLS), F32),
        ],
        compiler_params=pltpu.CompilerParams(
            dimension_semantics=("arbitrary", "arbitrary"),
            vmem_limit_bytes=VMEM_LIMIT_BYTES),
        name="mixer",
    )(*operands)


def _top16_rounds(scores, key_iota):
    cur = scores
    rank = jnp.full(scores.shape, float(PEER_TOPK), F32)
    tops = []
    for r in range(PEER_TOPK):
        m = jnp.max(cur, axis=0, keepdims=True)
        first = jnp.min(jnp.where(cur == m, key_iota, float(PEER_KEYS)), axis=0, keepdims=True)
        hit = key_iota == first
        cur = jnp.where(hit, -jnp.inf, cur)
        rank = jnp.where(hit, float(r), rank)
        tops.append(m)
    return rank, tops


def _sort_network_pairs(n):
    pairs = []
    p = 1
    while p < n:
        k = p
        while k >= 1:
            for j in range(k % p, n - k, 2 * k):
                for i in range(min(k, n - j - k)):
                    if (i + j) // (2 * p) == (i + j + k) // (2 * p):
                        pairs.append((i + j, i + j + k))
            k //= 2
        p *= 2
    return pairs


_SORT16 = _sort_network_pairs(PEER_TOPK)
F32_SUBLANES = 8


def _top16_sorted(scores):
    v = [scores[F32_SUBLANES * j:F32_SUBLANES * (j + 1)] for j in range(PEER_TOPK)]
    for (i, j) in _SORT16:
        v[i], v[j] = jnp.maximum(v[i], v[j]), jnp.minimum(v[i], v[j])
    shift = F32_SUBLANES // 2
    while shift >= 1:
        other = [pltpu.roll(x, shift, 0) for x in v]
        v = [jnp.maximum(v[j], other[PEER_TOPK - 1 - j]) for j in range(PEER_TOPK)]
        d = PEER_TOPK // 2
        while d >= 1:
            for j in range(PEER_TOPK):
                if (j // d) % 2 == 0:
                    v[j], v[j + d] = jnp.maximum(v[j], v[j + d]), jnp.minimum(v[j], v[j + d])
            d //= 2
        shift //= 2
    return v


def _route_kernel(h_ref, wqt_ref, keys_ref,
                  xt_ref, cnt_ref, e1_ref, rank2_ref, e2_ref,
                  qt_scr, key1_scr, e1_scr, tops_scr, csel_scr):
    tr = h_ref.shape[0]
    xt = h_ref[...].T.astype(BF16)
    xt_ref[...] = xt
    qt_scr[...] = _dot(wqt_ref[...], xt)

    key_iota = lax.broadcasted_iota(jnp.int32, (PEER_KEYS, LANES), 0).astype(F32)
    head_iota = lax.broadcasted_iota(jnp.int32, (PEER_HEADS, LANES), 0)
    tops_scr[...] = jnp.zeros(tops_scr.shape, F32)

    def stage1(exact_order):
        def head_body(h, tied):
            for k in range(2):
                q0 = pl.multiple_of((h * 2 + k) * PEER_HALF, PEER_HALF)
                qk = qt_scr[pl.ds(q0, PEER_HALF), :].astype(BF16)
                scores = _dot(keys_ref[k], qk)
                for lt in range(tr // LANES):
                    lanes = slice(lt * LANES, (lt + 1) * LANES)
                    sc = scores[:, lanes]
                    if exact_order:
                        rank, tops = _top16_rounds(sc, key_iota)
                        key1 = rank
                    else:
                        tops = _top16_sorted(sc)
                        at_least_last = jnp.where(sc >= tops[PEER_TOPK - 1][0:1], 1.0, 0.0)
                        n_sel = jnp.sum(at_least_last, axis=0, keepdims=True)
                        bad = jnp.abs(n_sel - float(PEER_TOPK))
                        for r in range(PEER_TOPK - 1):
                            bad = bad + jnp.where(tops[r][0:1] == tops[r + 1][0:1], 1.0, 0.0)
                        tied = jnp.maximum(tied, bad)
                        key1 = sc
                        if k == 1:
                            rank = jnp.full(sc.shape, float(PEER_TOPK), F32)
                            for r in reversed(range(PEER_TOPK)):
                                rank = jnp.where(sc >= tops[r][0:1], float(r), rank)
                    e = jnp.exp(sc - tops[0][0:1])
                    if k == 0:
                        key1_scr[h, :, lanes] = key1
                        e1_scr[h, :, lanes] = e
                    else:
                        rank2_ref[h, :, lanes] = rank.astype(BF16)
                        e2_ref[h, :, lanes] = e.astype(BF16)
                    for r in range(PEER_TOPK):
                        old = tops_scr[k, r, :, lanes]
                        tops_scr[k, r, :, lanes] = jnp.where(head_iota == h, tops[r], old)
            return tied

        return lax.fori_loop(0, PEER_HEADS, head_body, jnp.zeros((1, LANES), F32))

    any_tie = jnp.max(stage1(exact_order=False)) > 0.0

    @pl.when(any_tie)
    def _():
        stage1(exact_order=True)

    for lt in range(tr // LANES):
        lanes = slice(lt * LANES, (lt + 1) * LANES)
        s1 = [tops_scr[0, a, :, lanes] for a in range(PEER_TOPK)]
        s2 = [tops_scr[1, b, :, lanes] for b in range(PEER_TOPK)]
        sums = {ab: s1[ab[0]] + s2[ab[1]] for ab in _CANDIDATES}
        p1 = [jnp.exp(s1[a] - s1[0]) for a in range(PEER_TOPK)]
        p2 = [jnp.exp(s2[b] - s2[0]) for b in range(PEER_TOPK)]
        counts = [jnp.zeros((PEER_HEADS, LANES), F32) for _ in range(PEER_TOPK)]
        z = jnp.zeros((PEER_HEADS, LANES), F32)
        for (a, b) in _CANDIDATES:
            s = sums[(a, b)]
            beaten_by = jnp.full((PEER_HEADS, LANES), float((a + 1) * (b + 1) - 1), F32)
            for (a2, b2) in _CANDIDATES:
                if a2 < a and b2 > b:
                    beaten_by = beaten_by + jnp.where(sums[(a2, b2)] >= s, 1.0, 0.0)
                elif a2 > a and b2 < b:
                    beaten_by = beaten_by + jnp.where(sums[(a2, b2)] > s, 1.0, 0.0)
            sel = beaten_by < float(PEER_TOPK)
            counts[a] = counts[a] + jnp.where(sel, 1.0, 0.0)
            z = z + jnp.where(sel, p1[a] * p2[b], 0.0)
        for a in range(PEER_TOPK):
            csel_scr[a, :, lanes] = counts[a]
        csel_scr[PEER_TOPK, :, lanes] = 1.0 / z

    def stage3(keys_are_ranks):
        for h in range(PEER_HEADS):
            for lt in range(tr // LANES):
                lanes = slice(lt * LANES, (lt + 1) * LANES)
                key1 = key1_scr[h, :, lanes]
                cnt = jnp.zeros((PEER_KEYS, LANES), F32)
                for a in range(PEER_TOPK):
                    target = float(a) if keys_are_ranks else tops_scr[0, a, h:h + 1, lanes]
                    cnt = jnp.where(key1 == target, csel_scr[a, h:h + 1, lanes], cnt)
                cnt_ref[h, :, lanes] = cnt
                e1_ref[h, :, lanes] = e1_scr[h, :, lanes] * csel_scr[PEER_TOPK, h:h + 1, lanes]

    @pl.when(any_tie)
    def _():
        stage3(keys_are_ranks=True)

    @pl.when(jnp.logical_not(any_tie))
    def _():
        stage3(keys_are_ranks=False)


def _route(h1, w_q, sub_keys):
    t = h1.shape[0]
    tr = min(ROUTE_TOKENS, t)
    assert t % tr == 0 and tr % LANES == 0
    wqt = w_q.T.astype(BF16)
    keys = sub_keys.astype(BF16)
    nq = wqt.shape[0]
    table = lambda dt: jax.ShapeDtypeStruct((PEER_HEADS, PEER_KEYS, t), dt)
    table_spec = pl.BlockSpec((PEER_HEADS, PEER_KEYS, tr), lambda i: (0, 0, i))
    return pl.pallas_call(
        _route_kernel,
        grid=(t // tr,),
        in_specs=[
            pl.BlockSpec((tr, D_MODEL), lambda i: (i, 0)),
            pl.BlockSpec(wqt.shape, lambda i: (0, 0), pipeline_mode=pl.Buffered(1)),
            pl.BlockSpec(keys.shape, lambda i: (0, 0, 0), pipeline_mode=pl.Buffered(1)),
        ],
        out_specs=[
            pl.BlockSpec((D_MODEL, tr), lambda i: (0, i)),
            table_spec, table_spec, table_spec, table_spec,
        ],
        out_shape=[
            jax.ShapeDtypeStruct((D_MODEL, t), BF16),
            table(F32), table(F32), table(BF16), table(BF16),
        ],
        scratch_shapes=[
            pltpu.VMEM((nq, tr), F32),
            pltpu.VMEM((PEER_HEADS, PEER_KEYS, tr), F32),
            pltpu.VMEM((PEER_HEADS, PEER_KEYS, tr), F32),
            pltpu.VMEM((2, PEER_TOPK, PEER_HEADS, tr), F32),
            pltpu.VMEM((PEER_TOPK + 1, PEER_HEADS, tr), F32),
        ],
        compiler_params=pltpu.CompilerParams(
            dimension_semantics=("arbitrary",),
            vmem_limit_bytes=VMEM_LIMIT_BYTES),
        name="route",
    )(h1, wqt, keys)


def _expert_kernel(xt_ref, cnt_ref, e1_ref, rank2_ref, e2_ref, u_ref, vt_ref,
                   h1_ref, ln2g_ref, ln2b_ref,
                   out_ref,
                   acc_scr, act_scr, pre_a, pre_b):
    e_idx = pl.program_id(1)
    n_blocks = pl.num_programs(1) - 1
    sqrt_half = math.sqrt(0.5)

    def pre_activations(dst):
        dst[...] = _dot(u_ref[...], xt_ref[...]).astype(BF16)

    def row_to_keys(row):
        tile = jnp.broadcast_to(row, (BF16_SUBLANES, row.shape[1])).astype(BF16)
        reps = PEER_KEYS // BF16_SUBLANES
        return jnp.broadcast_to(tile[None], (reps,) + tile.shape).reshape(PEER_KEYS, row.shape[1])

    def finish_block(src):
        for i in range(EXPERT_ROWS):
            rows = slice(i * PEER_KEYS, (i + 1) * PEER_KEYS)
            hi = src[rows, :]
            gelu = 0.5 * hi * (1.0 + lax.erf(hi * sqrt_half))
            w = None
            for h in range(PEER_HEADS):
                cnt = row_to_keys(cnt_ref[h, i:i + 1, :])
                e1 = row_to_keys(e1_ref[h, i:i + 1, :])
                term = jnp.where(rank2_ref[h] < cnt, e2_ref[h], jnp.zeros((), BF16)) * e1
                w = term if w is None else w + term
            act_scr[rows, :] = gelu * w
        return _dot(vt_ref[...], act_scr[...])

    @pl.when(e_idx == 0)
    def _():
        acc_scr[...] = jnp.zeros(acc_scr.shape, F32)
        pre_activations(pre_a)

    for parity, (dst, src) in enumerate(((pre_a, pre_b), (pre_b, pre_a))):
        @pl.when((e_idx > 0) & (e_idx < n_blocks) & (e_idx % 2 == parity))
        def _(dst=dst, src=src):
            pre_activations(dst)
            acc_scr[...] += finish_block(src)

    @pl.when(e_idx == n_blocks)
    def _():
        total = acc_scr[...] + finish_block(pre_a if EXPERT_LAST_IN_A else pre_b)
        res = DN_ALPHA * h1_ref[...] + total.T
        out_ref[...] = _layer_norm(res, ln2g_ref[...], ln2b_ref[...])


def _experts(h1, xt, cnt, e1, rank2, e2, expert_u, expert_v, ln2_g, ln2_b):
    t = h1.shape[0]
    tm = min(EXPERT_TOKENS, t)
    assert t % tm == 0
    n_exp = expert_u.shape[0]
    eb = EXPERT_ROWS * PEER_KEYS
    assert n_exp == EXPERT_BLOCKS * eb
    u = expert_u.astype(BF16)
    vt = expert_v.astype(BF16).reshape(EXPERT_BLOCKS, eb, D_MODEL).transpose(0, 2, 1)
    last = EXPERT_BLOCKS - 1
    prev = lambda e: jnp.maximum(e - 1, 0)
    tab_full = pl.BlockSpec((PEER_HEADS, PEER_KEYS, tm), lambda i, e: (0, 0, i))
    tab_rows = pl.BlockSpec((PEER_HEADS, EXPERT_ROWS, tm), lambda i, e: (0, prev(e), i))
    vec = lambda v: v.reshape(1, -1).astype(F32)
    return pl.pallas_call(
        _expert_kernel,
        grid=(t // tm, EXPERT_BLOCKS + 1),
        in_specs=[
            pl.BlockSpec((D_MODEL, tm), lambda i, e: (0, i)),
            tab_rows, tab_rows, tab_full, tab_full,
            pl.BlockSpec((eb, D_MODEL), lambda i, e: (jnp.minimum(e, last), 0)),
            pl.BlockSpec((None, D_MODEL, eb), lambda i, e: (prev(e), 0, 0)),
            pl.BlockSpec((tm, D_MODEL), lambda i, e: (i, 0), pipeline_mode=pl.Buffered(1)),
            pl.BlockSpec((1, D_MODEL), lambda i, e: (0, 0)),
            pl.BlockSpec((1, D_MODEL), lambda i, e: (0, 0)),
        ],
        out_specs=pl.BlockSpec((tm, D_MODEL), lambda i, e: (i, 0)),
        out_shape=jax.ShapeDtypeStruct((t, D_MODEL), F32),
        scratch_shapes=[
            pltpu.VMEM((D_MODEL, tm), F32),
            pltpu.VMEM((eb, tm), BF16),
            pltpu.VMEM((eb, tm), BF16),
            pltpu.VMEM((eb, tm), BF16),
        ],
        compiler_params=pltpu.CompilerParams(
            dimension_semantics=("arbitrary", "arbitrary"),
            vmem_limit_bytes=VMEM_LIMIT_BYTES),
        name="experts",
    )(xt, cnt, e1, rank2, e2, u, vt, h1, vec(ln2_g), vec(ln2_b))


def kernel(x, w_in, pool_w, pool_scale, conv_w, conv_b, dt_bias, a_log, d_skip, ssm_norm_w,
           ssm_out, w_out, ln1_g, ln1_b, w_q, sub_keys, expert_u, expert_v, ln2_g, ln2_b):
    bsz, seq, d = x.shape
    h = x
    for i in range(DEPTH):
        h1 = _mixer(h, w_in[i], pool_w[i], pool_scale[i], conv_w[i], conv_b[i], dt_bias[i],
                    a_log[i], d_skip[i], ssm_norm_w[i], ssm_out[i], w_out[i], ln1_g[i], ln1_b[i])
        h1 = h1.reshape(bsz * seq, d)
        xt, cnt, e1, rank2, e2 = _route(h1, w_q[i], sub_keys[i])
        h = _experts(h1, xt, cnt, e1, rank2, e2, expert_u[i], expert_v[i], ln2_g[i], ln2_b[i])
        h = h.reshape(bsz, seq, d)
    return h
```

```python
import functools
import math

import numpy as np
import jax
import jax.numpy as jnp
from jax import lax
from jax.experimental import pallas as pl
from jax.experimental.pallas import tpu as pltpu

F32 = jnp.float32
BF16 = jnp.bfloat16

D_MODEL = 1024
POOL_WINDOWS = (2, 4, 8, 16)
POOL_GROUPS = len(POOL_WINDOWS)
POOL_GROUP_DIM = D_MODEL // POOL_GROUPS
POOL_HISTORY = 16
SSM_INNER = 2 * D_MODEL
SSM_HEAD_DIM = 64
SSM_HEADS = SSM_INNER // SSM_HEAD_DIM
SSM_GROUPS = 8
SSM_HEADS_PER_GROUP = SSM_HEADS // SSM_GROUPS
SSM_STATE = 128
SSM_CONV = 4
SSM_CHUNK = 128
SSM_BC = SSM_GROUPS * SSM_STATE
SSM_CONV_DIM = SSM_INNER + 2 * SSM_BC
SSM_NORM_GROUP = SSM_INNER // SSM_GROUPS
SSM_GROUP_COLS = SSM_HEADS_PER_GROUP * SSM_HEAD_DIM
CONV_HISTORY = 8
PEER_HEADS = 8
PEER_KEYS = 128
PEER_HALF = 128
PEER_TOPK = 16
DEPTH = 1
DN_ALPHA = (2.0 * DEPTH) ** 0.25
LN_EPS = 1e-5
RMS_EPS = 1e-5

LANES = 128
BF16_SUBLANES = 16
VMEM_LIMIT_BYTES = 56 * 1024 * 1024

MIXER_TOKENS = 256
XBC_COL_BLOCK = 512
ROUTE_TOKENS = 256
EXPERT_TOKENS = 512
EXPERT_ROWS = 16
EXPERT_BLOCKS = PEER_KEYS // EXPERT_ROWS
EXPERT_LAST_IN_A = (EXPERT_BLOCKS - 1) % 2 == 0

_CANDIDATES = [(a, b) for a in range(PEER_TOPK) for b in range(PEER_TOPK)
               if (a + 1) * (b + 1) <= PEER_TOPK]


def _dot(a, b):
    return jnp.dot(a, b, preferred_element_type=F32)


def _split_dot_left(onehot, v, terms):
    acc = None
    rem = v
    for _ in range(terms):
        part = rem.astype(BF16)
        d = _dot(onehot, part)
        acc = d if acc is None else acc + d
        rem = rem - part.astype(F32)
    return acc


def _sigmoid(x):
    return 1.0 / (1.0 + jnp.exp(-x))


def _silu(x):
    return x * _sigmoid(x)


def _layer_norm(x, g, b):
    mu = jnp.mean(x, axis=-1, keepdims=True)
    xc = x - mu
    var = jnp.mean(xc * xc, axis=-1, keepdims=True)
    return xc * lax.rsqrt(var + LN_EPS) * g + b


def _mixer_kernel(x_ref, wpool_ref, wz_ref, wxbc_ref, wdt_ref, wgp_ref, wgs_ref,
                  poolw_ref, pscale_ref, convw_ref, convb_ref, dtb_ref, alog_ref, dskip_ref,
                  normw_ref, ssmout_ref, wout_ref, ln1g_ref, ln1b_ref,
                  ltri_ref, ehead_ref,
                  out_ref,
                  pool_ext, xbc_ext, xbc_act, y_scr, a_scr, state):
    tq = x_ref.shape[1]
    s_idx = pl.program_id(1)

    @pl.when(s_idx == 0)
    def _():
        pool_ext[0:POOL_HISTORY, :] = jnp.zeros((POOL_HISTORY, D_MODEL), F32)
        xbc_ext[0:CONV_HISTORY, :] = jnp.zeros((CONV_HISTORY, SSM_CONV_DIM), F32)
        state[...] = jnp.zeros(state.shape, F32)

    x = x_ref[0]
    xb = x.astype(BF16)

    pool_ext[POOL_HISTORY:POOL_HISTORY + tq, :] = _dot(xb, wpool_ref[...])
    pos = (s_idx * tq + lax.broadcasted_iota(jnp.int32, (tq, 1), 0) + 1).astype(F32)
    for g, w in enumerate(POOL_WINDOWS):
        cols = slice(g * POOL_GROUP_DIM, (g + 1) * POOL_GROUP_DIM)
        ext = pool_ext[0:POOL_HISTORY + tq, cols]
        u = ext[POOL_HISTORY:]
        acc = ext
        span = 1
        while span < w:
            acc = acc + pltpu.roll(acc, span, 0)
            span *= 2
        pooled = acc[POOL_HISTORY:] / jnp.minimum(pos, float(w)) - u
        mixed = _dot(pooled.astype(BF16), poolw_ref[g])
        a_scr[:, cols] = mixed * pscale_ref[:, cols]
    pool_ext[0:POOL_HISTORY, :] = pool_ext[tq:tq + POOL_HISTORY, :]

    for cb in range(SSM_CONV_DIM // XBC_COL_BLOCK):
        cols = slice(cb * XBC_COL_BLOCK, (cb + 1) * XBC_COL_BLOCK)
        xbc_ext[CONV_HISTORY:CONV_HISTORY + tq, cols] = _dot(xb, wxbc_ref[:, cols])
        ext = xbc_ext[0:CONV_HISTORY + tq, cols]
        conv = convb_ref[:, cols] + convw_ref[SSM_CONV - 1:SSM_CONV, cols] * ext[CONV_HISTORY:]
        for k in range(SSM_CONV - 1):
            back = SSM_CONV - 1 - k
            conv = conv + convw_ref[k:k + 1, cols] * pltpu.roll(ext, back, 0)[CONV_HISTORY:]
        xbc_act[:, cols] = _silu(conv)
    xbc_ext[0:CONV_HISTORY, :] = xbc_ext[tq:tq + CONV_HISTORY, :]

    dtr = _dot(xb, wdt_ref[...]) + dtb_ref[...]
    dt = jnp.maximum(dtr, 0.0) + jnp.log(1.0 + jnp.exp(-jnp.abs(dtr)))
    adt = dt * (-jnp.exp(alog_ref[...]))

    tri = (lax.broadcasted_iota(jnp.int32, (SSM_CHUNK, SSM_CHUNK), 0)
           >= lax.broadcasted_iota(jnp.int32, (SSM_CHUNK, SSM_CHUNK), 1))
    ltri = ltri_ref[...]
    ehead = ehead_ref[...]
    head_of_row = lax.broadcasted_iota(jnp.int32, (SSM_HEADS_PER_GROUP * SSM_CHUNK, SSM_GROUP_COLS), 0) // SSM_CHUNK
    head_of_col = lax.broadcasted_iota(jnp.int32, (SSM_HEADS_PER_GROUP * SSM_CHUNK, SSM_GROUP_COLS), 1) // SSM_HEAD_DIM
    own_head = head_of_row == head_of_col
    x_off, b_off, c_off = 0, SSM_INNER, SSM_INNER + SSM_BC

    for c in range(tq // SSM_CHUNK):
        rows = slice(c * SSM_CHUNK, (c + 1) * SSM_CHUNK)
        acum = _split_dot_left(ltri, adt[rows], 3)
        acum_t = acum.T
        alast = acum[SSM_CHUNK - 1:SSM_CHUNK, :]
        ea = jnp.exp(acum)
        ea_hi = ea.astype(BF16)
        ea_lo = (ea - ea_hi.astype(F32)).astype(BF16)
        stack = jnp.concatenate([dt[rows].astype(BF16), jnp.exp(alast - acum).astype(BF16),
                                 ea_hi, ea_lo], axis=0)
        expanded = _dot(stack, ehead)
        dt_e = expanded[0:SSM_CHUNK]
        dec_e = expanded[SSM_CHUNK:2 * SSM_CHUNK]
        ea_e = expanded[2 * SSM_CHUNK:3 * SSM_CHUNK] + expanded[3 * SSM_CHUNK:4 * SSM_CHUNK]

        for g in range(SSM_GROUPS):
            gcols = slice(g * SSM_GROUP_COLS, (g + 1) * SSM_GROUP_COLS)
            ncols_b = slice(b_off + g * SSM_STATE, b_off + (g + 1) * SSM_STATE)
            ncols_c = slice(c_off + g * SSM_STATE, c_off + (g + 1) * SSM_STATE)
            cg = xbc_act[rows, ncols_c].astype(BF16)
            bg_t = xbc_act[rows, ncols_b].T.astype(BF16)
            cb_mat = _dot(cg, bg_t)
            xs_g = xbc_act[rows, x_off + g * SSM_GROUP_COLS:x_off + (g + 1) * SSM_GROUP_COLS]
            xd_g = xs_g * dt_e[:, gcols]
            g_parts = []
            for r in range(SSM_HEADS_PER_GROUP):
                hd = g * SSM_HEADS_PER_GROUP + r
                diff = acum[:, hd:hd + 1] - acum_t[hd:hd + 1, :]
                lmat = jnp.where(tri, jnp.exp(diff), 0.0)
                g_parts.append((cb_mat * lmat).astype(BF16))
            xd_b = xd_g.astype(BF16)
            xd_stack = jnp.where(own_head, jnp.concatenate([xd_b] * SSM_HEADS_PER_GROUP, axis=0),
                                 jnp.zeros((), BF16))
            y_diag = _dot(jnp.concatenate(g_parts, axis=1), xd_stack)
            st = state[g]
            y_off = _dot(cg, st.astype(BF16)) * ea_e[:, gcols]
            y_scr[rows, gcols] = y_diag + y_off + dskip_ref[:, gcols] * xs_g
            upd = _dot(bg_t, (xd_g * dec_e[:, gcols]).astype(BF16))
            state[g] = st * ea_e[SSM_CHUNK - 1:SSM_CHUNK, gcols] + upd

    y = y_scr[...] * _silu(_dot(xb, wz_ref[...]))
    for g in range(SSM_GROUPS):
        ncols = slice(g * SSM_NORM_GROUP, (g + 1) * SSM_NORM_GROUP)
        yg = y[:, ncols]
        ms = jnp.mean(yg * yg, axis=-1, keepdims=True)
        y_scr[:, ncols] = yg * lax.rsqrt(ms + RMS_EPS) * normw_ref[:, ncols]
    m_out = _dot(y_scr[...].astype(BF16), ssmout_ref[...])

    merged = (_sigmoid(_dot(xb, wgp_ref[...])) * a_scr[...]
              + _sigmoid(_dot(xb, wgs_ref[...])) * m_out)
    mix = _dot(merged.astype(BF16), wout_ref[...])
    out_ref[0] = _layer_norm(DN_ALPHA * x + mix, ln1g_ref[...], ln1b_ref[...])


def _resident(shape):
    zeros = (0,) * len(shape)
    return pl.BlockSpec(shape, lambda b, s: zeros, pipeline_mode=pl.Buffered(1))


def _mixer(x, w_in, pool_w, pool_scale, conv_w, conv_b, dt_bias, a_log, d_skip, ssm_norm_w,
           ssm_out, w_out, ln1_g, ln1_b):
    bsz, seq, _ = x.shape
    tq = min(MIXER_TOKENS, seq)
    assert seq % tq == 0 and tq % SSM_CHUNK == 0
    o1 = D_MODEL
    o2 = o1 + SSM_INNER
    o3 = o2 + SSM_CONV_DIM
    o4 = o3 + SSM_HEADS
    o5 = o4 + D_MODEL
    pad_h = LANES - SSM_HEADS
    w_pool = w_in[:, :o1].astype(BF16)
    w_z = w_in[:, o1:o2].astype(BF16)
    w_xbc = w_in[:, o2:o3].astype(BF16)
    w_dt = jnp.pad(w_in[:, o3:o4], ((0, 0), (0, pad_h))).astype(BF16)
    w_gp = w_in[:, o4:o5].astype(BF16)
    w_gs = w_in[:, o5:].astype(BF16)
    row = lambda v: v.reshape(1, -1).astype(F32)
    dtb = jnp.pad(row(dt_bias), ((0, 0), (0, pad_h)))
    alog = jnp.pad(row(a_log), ((0, 0), (0, pad_h)))
    dskip = jnp.repeat(row(d_skip), SSM_HEAD_DIM, axis=1)

    q = SSM_CHUNK
    ltri = jnp.asarray(np.tril(np.ones((q, q), np.float32)), BF16)
    eh = np.zeros((LANES, SSM_INNER), np.float32)
    for h in range(SSM_HEADS):
        eh[h, h * SSM_HEAD_DIM:(h + 1) * SSM_HEAD_DIM] = 1.0
    ehead = jnp.asarray(eh, BF16)

    operands = [
        x, w_pool, w_z, w_xbc, w_dt, w_gp, w_gs,
        pool_w.astype(BF16), row(pool_scale), conv_w.astype(F32), row(conv_b), dtb, alog, dskip,
        row(ssm_norm_w), ssm_out.astype(BF16), w_out.astype(BF16), row(ln1_g), row(ln1_b),
        ltri, ehead,
    ]
    in_specs = [pl.BlockSpec((1, tq, D_MODEL), lambda b, s: (b, s, 0))]
    in_specs += [_resident(op.shape) for op in operands[1:]]
    return pl.pallas_call(
        _mixer_kernel,
        grid=(bsz, seq // tq),
        in_specs=in_specs,
        out_specs=pl.BlockSpec((1, tq, D_MODEL), lambda b, s: (b, s, 0)),
        out_shape=jax.ShapeDtypeStruct(x.shape, F32),
        scratch_shapes=[
            pltpu.VMEM((POOL_HISTORY + tq, D_MODEL), F32),
            pltpu.VMEM((CONV_HISTORY + tq, SSM_CONV_DIM), F32),
            pltpu.VMEM((tq, SSM_CONV_DIM), F32),
            pltpu.VMEM((tq, SSM_INNER), F32),
            pltpu.VMEM((tq, D_MODEL), F32),
            pltpu.VMEM((SSM_GROUPS, SSM_STATE, SSM_GROUP_COLS), F32),
        ],
        compiler_params=pltpu.CompilerParams(
            dimension_semantics=("arbitrary", "arbitrary"),
            vmem_limit_bytes=VMEM_LIMIT_BYTES),
        name="mixer",
    )(*operands)


def _top16_rounds(scores, key_iota):
    cur = scores
    rank = jnp.full(scores.shape, float(PEER_TOPK), F32)
    tops = []
    for r in range(PEER_TOPK):
        m = jnp.max(cur, axis=0, keepdims=True)
        first = jnp.min(jnp.where(cur == m, key_iota, float(PEER_KEYS)), axis=0, keepdims=True)
        hit = key_iota == first
        cur = jnp.where(hit, -jnp.inf, cur)
        rank = jnp.where(hit, float(r), rank)
        tops.append(m)
    return rank, tops


def _sort_network_pairs(n):
    pairs = []
    p = 1
    while p < n:
        k = p
        while k >= 1:
            for j in range(k % p, n - k, 2 * k):
                for i in range(min(k, n - j - k)):
                    if (i + j) // (2 * p) == (i + j + k) // (2 * p):
                        pairs.append((i + j, i + j + k))
            k //= 2
        p *= 2
    return pairs


_SORT16 = _sort_network_pairs(PEER_TOPK)
F32_SUBLANES = 8


def _top16_sorted(scores):
    v = [scores[F32_SUBLANES * j:F32_SUBLANES * (j + 1)] for j in range(PEER_TOPK)]
    for (i, j) in _SORT16:
        v[i], v[j] = jnp.maximum(v[i], v[j]), jnp.minimum(v[i], v[j])
    shift = F32_SUBLANES // 2
    while shift >= 1:
        other = [pltpu.roll(x, shift, 0) for x in v]
        v = [jnp.maximum(v[j], other[PEER_TOPK - 1 - j]) for j in range(PEER_TOPK)]
        d = PEER_TOPK // 2
        while d >= 1:
            for j in range(PEER_TOPK):
                if (j // d) % 2 == 0:
                    v[j], v[j + d] = jnp.maximum(v[j], v[j + d]), jnp.minimum(v[j], v[j + d])
            d //= 2
        shift //= 2
    return v


def _route_kernel(h_ref, wqt_ref, keys_ref,
                  xt_ref, cnt_ref, e1_ref, rank2_ref, e2_ref,
                  qt_scr, key1_scr, e1_scr, tops_scr, csel_scr):
    tr = h_ref.shape[0]
    xt = h_ref[...].T.astype(BF16)
    xt_ref[...] = xt
    qt_scr[...] = _dot(wqt_ref[...], xt)

    key_iota = lax.broadcasted_iota(jnp.int32, (PEER_KEYS, LANES), 0).astype(F32)
    head_iota = lax.broadcasted_iota(jnp.int32, (PEER_HEADS, LANES), 0)
    tops_scr[...] = jnp.zeros(tops_scr.shape, F32)

    def stage1(exact_order):
        def head_body(h, tied):
            for k in range(2):
                q0 = pl.multiple_of((h * 2 + k) * PEER_HALF, PEER_HALF)
                qk = qt_scr[pl.ds(q0, PEER_HALF), :].astype(BF16)
                scores = _dot(keys_ref[k], qk)
                for lt in range(tr // LANES):
                    lanes = slice(lt * LANES, (lt + 1) * LANES)
                    sc = scores[:, lanes]
                    if exact_order:
                        rank, tops = _top16_rounds(sc, key_iota)
                        key1 = rank
                    else:
                        tops = _top16_sorted(sc)
                        at_least_last = jnp.where(sc >= tops[PEER_TOPK - 1][0:1], 1.0, 0.0)
                        n_sel = jnp.sum(at_least_last, axis=0, keepdims=True)
                        bad = jnp.abs(n_sel - float(PEER_TOPK))
                        for r in range(PEER_TOPK - 1):
                            bad = bad + jnp.where(tops[r][0:1] == tops[r + 1][0:1], 1.0, 0.0)
                        tied = jnp.maximum(tied, bad)
                        key1 = sc
                        if k == 1:
                            rank = jnp.full(sc.shape, float(PEER_TOPK), F32)
                            for r in reversed(range(PEER_TOPK)):
                                rank = jnp.where(sc >= tops[r][0:1], float(r), rank)
                    e = jnp.exp(sc - tops[0][0:1])
                    if k == 0:
                        key1_scr[h, :, lanes] = key1
                        e1_scr[h, :, lanes] = e
                    else:
                        rank2_ref[h, :, lanes] = rank.astype(BF16)
                        e2_ref[h, :, lanes] = e.astype(BF16)
                    for r in range(PEER_TOPK):
                        old = tops_scr[k, r, :, lanes]
                        tops_scr[k, r, :, lanes] = jnp.where(head_iota == h, tops[r], old)
            return tied

        return lax.fori_loop(0, PEER_HEADS, head_body, jnp.zeros((1, LANES), F32))

    any_tie = jnp.max(stage1(exact_order=False)) > 0.0

    @pl.when(any_tie)
    def _():
        stage1(exact_order=True)

    for lt in range(tr // LANES):
        lanes = slice(lt * LANES, (lt + 1) * LANES)
        s1 = [tops_scr[0, a, :, lanes] for a in range(PEER_TOPK)]
        s2 = [tops_scr[1, b, :, lanes] for b in range(PEER_TOPK)]
        sums = {ab: s1[ab[0]] + s2[ab[1]] for ab in _CANDIDATES}
        p1 = [jnp.exp(s1[a] - s1[0]) for a in range(PEER_TOPK)]
        p2 = [jnp.exp(s2[b] - s2[0]) for b in range(PEER_TOPK)]
        counts = [jnp.zeros((PEER_HEADS, LANES), F32) for _ in range(PEER_TOPK)]
        z = jnp.zeros((PEER_HEADS, LANES), F32)
        for (a, b) in _CANDIDATES:
            s = sums[(a, b)]
            beaten_by = jnp.full((PEER_HEADS, LANES), float((a + 1) * (b + 1) - 1), F32)
            for (a2, b2) in _CANDIDATES:
                if a2 < a and b2 > b:
                    beaten_by = beaten_by + jnp.where(sums[(a2, b2)] >= s, 1.0, 0.0)
                elif a2 > a and b2 < b:
                    beaten_by = beaten_by + jnp.where(sums[(a2, b2)] > s, 1.0, 0.0)
            sel = beaten_by < float(PEER_TOPK)
            counts[a] = counts[a] + jnp.where(sel, 1.0, 0.0)
            z = z + jnp.where(sel, p1[a] * p2[b], 0.0)
        for a in range(PEER_TOPK):
            csel_scr[a, :, lanes] = counts[a]
        csel_scr[PEER_TOPK, :, lanes] = 1.0 / z

    def stage3(keys_are_ranks):
        for h in range(PEER_HEADS):
            for lt in range(tr // LANES):
                lanes = slice(lt * LANES, (lt + 1) * LANES)
                key1 = key1_scr[h, :, lanes]
                cnt = jnp.zeros((PEER_KEYS, LANES), F32)
                for a in range(PEER_TOPK):
                    target = float(a) if keys_are_ranks else tops_scr[0, a, h:h + 1, lanes]
                    cnt = jnp.where(key1 == target, csel_scr[a, h:h + 1, lanes], cnt)
                cnt_ref[h, :, lanes] = cnt
                e1_ref[h, :, lanes] = e1_scr[h, :, lanes] * csel_scr[PEER_TOPK, h:h + 1, lanes]

    @pl.when(any_tie)
    def _():
        stage3(keys_are_ranks=True)

    @pl.when(jnp.logical_not(any_tie))
    def _():
        stage3(keys_are_ranks=False)


def _route(h1, w_q, sub_keys):
    t = h1.shape[0]
    tr = min(ROUTE_TOKENS, t)
    assert t % tr == 0 and tr % LANES == 0
    wqt = w_q.T.astype(BF16)
    keys = sub_keys.astype(BF16)
    nq = wqt.shape[0]
    table = lambda dt: jax.ShapeDtypeStruct((PEER_HEADS, PEER_KEYS, t), dt)
    table_spec = pl.BlockSpec((PEER_HEADS, PEER_KEYS, tr), lambda i: (0, 0, i))
    return pl.pallas_call(
        _route_kernel,
        grid=(t // tr,),
        in_specs=[
            pl.BlockSpec((tr, D_MODEL), lambda i: (i, 0)),
            pl.BlockSpec(wqt.shape, lambda i: (0, 0), pipeline_mode=pl.Buffered(1)),
            pl.BlockSpec(keys.shape, lambda i: (0, 0, 0), pipeline_mode=pl.Buffered(1)),
        ],
        out_specs=[
            pl.BlockSpec((D_MODEL, tr), lambda i: (0, i)),
            table_spec, table_spec, table_spec, table_spec,
        ],
        out_shape=[
            jax.ShapeDtypeStruct((D_MODEL, t), BF16),
            table(F32), table(F32), table(BF16), table(BF16),
        ],
        scratch_shapes=[
            pltpu.VMEM((nq, tr), F32),
            pltpu.VMEM((PEER_HEADS, PEER_KEYS, tr), F32),
            pltpu.VMEM((PEER_HEADS, PEER_KEYS, tr), F32),
            pltpu.VMEM((2, PEER_TOPK, PEER_HEADS, tr), F32),
            pltpu.VMEM((PEER_TOPK + 1, PEER_HEADS, tr), F32),
        ],
        compiler_params=pltpu.CompilerParams(
            dimension_semantics=("arbitrary",),
            vmem_limit_bytes=VMEM_LIMIT_BYTES),
        name="route",
    )(h1, wqt, keys)


def _expert_kernel(xt_ref, cnt_ref, e1_ref, rank2_ref, e2_ref, u_ref, vt_ref,
                   h1_ref, ln2g_ref, ln2b_ref,
                   out_ref,
                   acc_scr, act_scr, pre_a, pre_b):
    e_idx = pl.program_id(1)
    n_blocks = pl.num_programs(1) - 1
    sqrt_half = math.sqrt(0.5)

    def pre_activations(dst):
        dst[...] = _dot(u_ref[...], xt_ref[...]).astype(BF16)

    def row_to_keys(row):
        tile = jnp.broadcast_to(row, (BF16_SUBLANES, row.shape[1])).astype(BF16)
        reps = PEER_KEYS // BF16_SUBLANES
        return jnp.broadcast_to(tile[None], (reps,) + tile.shape).reshape(PEER_KEYS, row.shape[1])

    def finish_block(src):
        for i in range(EXPERT_ROWS):
            rows = slice(i * PEER_KEYS, (i + 1) * PEER_KEYS)
            hi = src[rows, :]
            gelu = 0.5 * hi * (1.0 + lax.erf(hi * sqrt_half))
            w = None
            for h in range(PEER_HEADS):
                cnt = row_to_keys(cnt_ref[h, i:i + 1, :])
                e1 = row_to_keys(e1_ref[h, i:i + 1, :])
                term = jnp.where(rank2_ref[h] < cnt, e2_ref[h], jnp.zeros((), BF16)) * e1
                w = term if w is None else w + term
            act_scr[rows, :] = gelu * w
        return _dot(vt_ref[...], act_scr[...])

    @pl.when(e_idx == 0)
    def _():
        acc_scr[...] = jnp.zeros(acc_scr.shape, F32)
        pre_activations(pre_a)

    for parity, (dst, src) in enumerate(((pre_a, pre_b), (pre_b, pre_a))):
        @pl.when((e_idx > 0) & (e_idx < n_blocks) & (e_idx % 2 == parity))
        def _(dst=dst, src=src):
            pre_activations(dst)
            acc_scr[...] += finish_block(src)

    @pl.when(e_idx == n_blocks)
    def _():
        total = acc_scr[...] + finish_block(pre_a if EXPERT_LAST_IN_A else pre_b)
        res = DN_ALPHA * h1_ref[...] + total.T
        out_ref[...] = _layer_norm(res, ln2g_ref[...], ln2b_ref[...])


def _experts(h1, xt, cnt, e1, rank2, e2, expert_u, expert_v, ln2_g, ln2_b):
    t = h1.shape[0]
    tm = min(EXPERT_TOKENS, t)
    assert t % tm == 0
    n_exp = expert_u.shape[0]
    eb = EXPERT_ROWS * PEER_KEYS
    assert n_exp == EXPERT_BLOCKS * eb
    u = expert_u.astype(BF16)
    vt = expert_v.astype(BF16).reshape(EXPERT_BLOCKS, eb, D_MODEL).transpose(0, 2, 1)
    last = EXPERT_BLOCKS - 1
    prev = lambda e: jnp.maximum(e - 1, 0)
    tab_full = pl.BlockSpec((PEER_HEADS, PEER_KEYS, tm), lambda i, e: (0, 0, i))
    tab_rows = pl.BlockSpec((PEER_HEADS, EXPERT_ROWS, tm), lambda i, e: (0, prev(e), i))
    vec = lambda v: v.reshape(1, -1).astype(F32)
    return pl.pallas_call(
        _expert_kernel,
        grid=(t // tm, EXPERT_BLOCKS + 1),
        in_specs=[
            pl.BlockSpec((D_MODEL, tm), lambda i, e: (0, i)),
            tab_rows, tab_rows, tab_full, tab_full,
            pl.BlockSpec((eb, D_MODEL), lambda i, e: (jnp.minimum(e, last), 0)),
            pl.BlockSpec((None, D_MODEL, eb), lambda i, e: (prev(e), 0, 0)),
            pl.BlockSpec((tm, D_MODEL), lambda i, e: (i, 0)),
            pl.BlockSpec((1, D_MODEL), lambda i, e: (0, 0)),
            pl.BlockSpec((1, D_MODEL), lambda i, e: (0, 0)),
        ],
        out_specs=pl.BlockSpec((tm, D_MODEL), lambda i, e: (i, 0)),
        out_shape=jax.ShapeDtypeStruct((t, D_MODEL), F32),
        scratch_shapes=[
            pltpu.VMEM((D_MODEL, tm), F32),
            pltpu.VMEM((eb, tm), BF16),
            pltpu.VMEM((eb, tm), BF16),
            pltpu.VMEM((eb, tm), BF16),
        ],
        compiler_params=pltpu.CompilerParams(
            dimension_semantics=("arbitrary", "arbitrary"),
            vmem_limit_bytes=VMEM_LIMIT_BYTES),
        name="experts",
    )(xt, cnt, e1, rank2, e2, u, vt, h1, vec(ln2_g), vec(ln2_b))


def kernel(x, w_in, pool_w, pool_scale, conv_w, conv_b, dt_bias, a_log, d_skip, ssm_norm_w,
           ssm_out, w_out, ln1_g, ln1_b, w_q, sub_keys, expert_u, expert_v, ln2_g, ln2_b):
    bsz, seq, d = x.shape
    h = x
    for i in range(DEPTH):
        h1 = _mixer(h, w_in[i], pool_w[i], pool_scale[i], conv_w[i], conv_b[i], dt_bias[i],
                    a_log[i], d_skip[i], ssm_norm_w[i], ssm_out[i], w_out[i], ln1_g[i], ln1_b[i])
        h1 = h1.reshape(bsz * seq, d)
        xt, cnt, e1, rank2, e2 = _route(h1, w_q[i], sub_keys[i])
        h = _experts(h1, xt, cnt, e1, rank2, e2, expert_u[i], expert_v[i], ln2_g[i], ln2_b[i])
        h = h.reshape(bsz, seq, d)
    return h
```

```python
import functools
import math

import numpy as np
import jax
import jax.numpy as jnp
from jax import lax
from jax.experimental import pallas as pl
from jax.experimental.pallas import tpu as pltpu

F32 = jnp.float32
BF16 = jnp.bfloat16

D_MODEL = 1024
POOL_WINDOWS = (2, 4, 8, 16)
POOL_GROUPS = len(POOL_WINDOWS)
POOL_GROUP_DIM = D_MODEL // POOL_GROUPS
POOL_HISTORY = 16
SSM_INNER = 2 * D_MODEL
SSM_HEAD_DIM = 64
SSM_HEADS = SSM_INNER // SSM_HEAD_DIM
SSM_GROUPS = 8
SSM_HEADS_PER_GROUP = SSM_HEADS // SSM_GROUPS
SSM_STATE = 128
SSM_CONV = 4
SSM_CHUNK = 128
SSM_BC = SSM_GROUPS * SSM_STATE
SSM_CONV_DIM = SSM_INNER + 2 * SSM_BC
SSM_NORM_GROUP = SSM_INNER // SSM_GROUPS
SSM_GROUP_COLS = SSM_HEADS_PER_GROUP * SSM_HEAD_DIM
CONV_HISTORY = 8
PEER_HEADS = 8
PEER_KEYS = 128
PEER_HALF = 128
PEER_TOPK = 16
DEPTH = 1
DN_ALPHA = (2.0 * DEPTH) ** 0.25
LN_EPS = 1e-5
RMS_EPS = 1e-5

LANES = 128
BF16_SUBLANES = 16
VMEM_LIMIT_BYTES = 56 * 1024 * 1024

MIXER_TOKENS = 256
XBC_COL_BLOCK = 512
ROUTE_TOKENS = 256
EXPERT_TOKENS = 512
EXPERT_ROWS = 16
EXPERT_BLOCKS = PEER_KEYS // EXPERT_ROWS
EXPERT_LAST_IN_A = (EXPERT_BLOCKS - 1) % 2 == 0

_CANDIDATES = [(a, b) for a in range(PEER_TOPK) for b in range(PEER_TOPK)
               if (a + 1) * (b + 1) <= PEER_TOPK]


def _dot(a, b):
    return jnp.dot(a, b, preferred_element_type=F32)


def _split_dot_left(onehot, v, terms):
    acc = None
    rem = v
    for _ in range(terms):
        part = rem.astype(BF16)
        d = _dot(onehot, part)
        acc = d if acc is None else acc + d
        rem = rem - part.astype(F32)
    return acc


def _sigmoid(x):
    return 1.0 / (1.0 + jnp.exp(-x))


def _silu(x):
    return x * _sigmoid(x)


def _layer_norm(x, g, b):
    mu = jnp.mean(x, axis=-1, keepdims=True)
    xc = x - mu
    var = jnp.mean(xc * xc, axis=-1, keepdims=True)
    return xc * lax.rsqrt(var + LN_EPS) * g + b


def _mixer_kernel(x_ref, wpool_ref, wz_ref, wxbc_ref, wdt_ref, wgp_ref, wgs_ref,
                  poolw_ref, pscale_ref, convw_ref, convb_ref, dtb_ref, alog_ref, dskip_ref,
                  normw_ref, ssmout_ref, wout_ref, ln1g_ref, ln1b_ref,
                  ltri_ref, ehead_ref,
                  out_ref,
                  pool_ext, xbc_ext, xbc_act, y_scr, a_scr, state):
    tq = x_ref.shape[1]
    s_idx = pl.program_id(1)

    @pl.when(s_idx == 0)
    def _():
        pool_ext[0:POOL_HISTORY, :] = jnp.zeros((POOL_HISTORY, D_MODEL), F32)
        xbc_ext[0:CONV_HISTORY, :] = jnp.zeros((CONV_HISTORY, SSM_CONV_DIM), F32)
        state[...] = jnp.zeros(state.shape, F32)

    x = x_ref[0]
    xb = x.astype(BF16)

    pool_ext[POOL_HISTORY:POOL_HISTORY + tq, :] = _dot(xb, wpool_ref[...])
    pos = (s_idx * tq + lax.broadcasted_iota(jnp.int32, (tq, 1), 0) + 1).astype(F32)
    for g, w in enumerate(POOL_WINDOWS):
        cols = slice(g * POOL_GROUP_DIM, (g + 1) * POOL_GROUP_DIM)
        ext = pool_ext[0:POOL_HISTORY + tq, cols]
        u = ext[POOL_HISTORY:]
        acc = ext
        span = 1
        while span < w:
            acc = acc + pltpu.roll(acc, span, 0)
            span *= 2
        pooled = acc[POOL_HISTORY:] / jnp.minimum(pos, float(w)) - u
        mixed = _dot(pooled.astype(BF16), poolw_ref[g])
        a_scr[:, cols] = mixed * pscale_ref[:, cols]
    pool_ext[0:POOL_HISTORY, :] = pool_ext[tq:tq + POOL_HISTORY, :]

    for cb in range(SSM_CONV_DIM // XBC_COL_BLOCK):
        cols = slice(cb * XBC_COL_BLOCK, (cb + 1) * XBC_COL_BLOCK)
        xbc_ext[CONV_HISTORY:CONV_HISTORY + tq, cols] = _dot(xb, wxbc_ref[:, cols])
        ext = xbc_ext[0:CONV_HISTORY + tq, cols]
        conv = convb_ref[:, cols] + convw_ref[SSM_CONV - 1:SSM_CONV, cols] * ext[CONV_HISTORY:]
        for k in range(SSM_CONV - 1):
            back = SSM_CONV - 1 - k
            conv = conv + convw_ref[k:k + 1, cols] * pltpu.roll(ext, back, 0)[CONV_HISTORY:]
        xbc_act[:, cols] = _silu(conv)
    xbc_ext[0:CONV_HISTORY, :] = xbc_ext[tq:tq + CONV_HISTORY, :]

    dtr = _dot(xb, wdt_ref[...]) + dtb_ref[...]
    dt = jnp.maximum(dtr, 0.0) + jnp.log(1.0 + jnp.exp(-jnp.abs(dtr)))
    adt = dt * (-jnp.exp(alog_ref[...]))

    tri = (lax.broadcasted_iota(jnp.int32, (SSM_CHUNK, SSM_CHUNK), 0)
           >= lax.broadcasted_iota(jnp.int32, (SSM_CHUNK, SSM_CHUNK), 1))
    ltri = ltri_ref[...]
    ehead = ehead_ref[...]
    head_of_row = lax.broadcasted_iota(jnp.int32, (SSM_HEADS_PER_GROUP * SSM_CHUNK, SSM_GROUP_COLS), 0) // SSM_CHUNK
    head_of_col = lax.broadcasted_iota(jnp.int32, (SSM_HEADS_PER_GROUP * SSM_CHUNK, SSM_GROUP_COLS), 1) // SSM_HEAD_DIM
    own_head = head_of_row == head_of_col
    x_off, b_off, c_off = 0, SSM_INNER, SSM_INNER + SSM_BC

    for c in range(tq // SSM_CHUNK):
        rows = slice(c * SSM_CHUNK, (c + 1) * SSM_CHUNK)
        acum = _split_dot_left(ltri, adt[rows], 3)
        acum_t = acum.T
        alast = acum[SSM_CHUNK - 1:SSM_CHUNK, :]
        ea = jnp.exp(acum)
        ea_hi = ea.astype(BF16)
        ea_lo = (ea - ea_hi.astype(F32)).astype(BF16)
        stack = jnp.concatenate([dt[rows].astype(BF16), jnp.exp(alast - acum).astype(BF16),
                                 ea_hi, ea_lo], axis=0)
        expanded = _dot(stack, ehead)
        dt_e = expanded[0:SSM_CHUNK]
        dec_e = expanded[SSM_CHUNK:2 * SSM_CHUNK]
        ea_e = expanded[2 * SSM_CHUNK:3 * SSM_CHUNK] + expanded[3 * SSM_CHUNK:4 * SSM_CHUNK]

        for g in range(SSM_GROUPS):
            gcols = slice(g * SSM_GROUP_COLS, (g + 1) * SSM_GROUP_COLS)
            ncols_b = slice(b_off + g * SSM_STATE, b_off + (g + 1) * SSM_STATE)
            ncols_c = slice(c_off + g * SSM_STATE, c_off + (g + 1) * SSM_STATE)
            cg = xbc_act[rows, ncols_c].astype(BF16)
            bg_t = xbc_act[rows, ncols_b].T.astype(BF16)
            cb_mat = _dot(cg, bg_t)
            xs_g = xbc_act[rows, x_off + g * SSM_GROUP_COLS:x_off + (g + 1) * SSM_GROUP_COLS]
            xd_g = xs_g * dt_e[:, gcols]
            g_parts = []
            for r in range(SSM_HEADS_PER_GROUP):
                hd = g * SSM_HEADS_PER_GROUP + r
                diff = acum[:, hd:hd + 1] - acum_t[hd:hd + 1, :]
                lmat = jnp.where(tri, jnp.exp(diff), 0.0)
                g_parts.append((cb_mat * lmat).astype(BF16))
            xd_b = xd_g.astype(BF16)
            xd_stack = jnp.where(own_head, jnp.concatenate([xd_b] * SSM_HEADS_PER_GROUP, axis=0),
                                 jnp.zeros((), BF16))
            y_diag = _dot(jnp.concatenate(g_parts, axis=1), xd_stack)
            st = state[g]
            y_off = _dot(cg, st.astype(BF16)) * ea_e[:, gcols]
            y_scr[rows, gcols] = y_diag + y_off + dskip_ref[:, gcols] * xs_g
            upd = _dot(bg_t, (xd_g * dec_e[:, gcols]).astype(BF16))
            state[g] = st * ea_e[SSM_CHUNK - 1:SSM_CHUNK, gcols] + upd

    y = y_scr[...] * _silu(_dot(xb, wz_ref[...]))
    for g in range(SSM_GROUPS):
        ncols = slice(g * SSM_NORM_GROUP, (g + 1) * SSM_NORM_GROUP)
        yg = y[:, ncols]
        ms = jnp.mean(yg * yg, axis=-1, keepdims=True)
        y_scr[:, ncols] = yg * lax.rsqrt(ms + RMS_EPS) * normw_ref[:, ncols]
    m_out = _dot(y_scr[...].astype(BF16), ssmout_ref[...])

    merged = (_sigmoid(_dot(xb, wgp_ref[...])) * a_scr[...]
              + _sigmoid(_dot(xb, wgs_ref[...])) * m_out)
    mix = _dot(merged.astype(BF16), wout_ref[...])
    out_ref[0] = _layer_norm(DN_ALPHA * x + mix, ln1g_ref[...], ln1b_ref[...])


def _resident(shape):
    zeros = (0,) * len(shape)
    return pl.BlockSpec(shape, lambda b, s: zeros, pipeline_mode=pl.Buffered(1))


def _mixer(x, w_in, pool_w, pool_scale, conv_w, conv_b, dt_bias, a_log, d_skip, ssm_norm_w,
           ssm_out, w_out, ln1_g, ln1_b):
    bsz, seq, _ = x.shape
    tq = min(MIXER_TOKENS, seq)
    assert seq % tq == 0 and tq % SSM_CHUNK == 0
    o1 = D_MODEL
    o2 = o1 + SSM_INNER
    o3 = o2 + SSM_CONV_DIM
    o4 = o3 + SSM_HEADS
    o5 = o4 + D_MODEL
    pad_h = LANES - SSM_HEADS
    w_pool = w_in[:, :o1].astype(BF16)
    w_z = w_in[:, o1:o2].astype(BF16)
    w_xbc = w_in[:, o2:o3].astype(BF16)
    w_dt = jnp.pad(w_in[:, o3:o4], ((0, 0), (0, pad_h))).astype(BF16)
    w_gp = w_in[:, o4:o5].astype(BF16)
    w_gs = w_in[:, o5:].astype(BF16)
    row = lambda v: v.reshape(1, -1).astype(F32)
    dtb = jnp.pad(row(dt_bias), ((0, 0), (0, pad_h)))
    alog = jnp.pad(row(a_log), ((0, 0), (0, pad_h)))
    dskip = jnp.repeat(row(d_skip), SSM_HEAD_DIM, axis=1)

    q = SSM_CHUNK
    ltri = jnp.asarray(np.tril(np.ones((q, q), np.float32)), BF16)
    eh = np.zeros((LANES, SSM_INNER), np.float32)
    for h in range(SSM_HEADS):
        eh[h, h * SSM_HEAD_DIM:(h + 1) * SSM_HEAD_DIM] = 1.0
    ehead = jnp.asarray(eh, BF16)

    operands = [
        x, w_pool, w_z, w_xbc, w_dt, w_gp, w_gs,
        pool_w.astype(BF16), row(pool_scale), conv_w.astype(F32), row(conv_b), dtb, alog, dskip,
        row(ssm_norm_w), ssm_out.astype(BF16), w_out.astype(BF16), row(ln1_g), row(ln1_b),
        ltri, ehead,
    ]
    in_specs = [pl.BlockSpec((1, tq, D_MODEL), lambda b, s: (b, s, 0))]
    in_specs += [_resident(op.shape) for op in operands[1:]]
    return pl.pallas_call(
        _mixer_kernel,
        grid=(bsz, seq // tq),
        in_specs=in_specs,
        out_specs=pl.BlockSpec((1, tq, D_MODEL), lambda b, s: (b, s, 0)),
        out_shape=jax.ShapeDtypeStruct(x.shape, F32),
        scratch_shapes=[
            pltpu.VMEM((POOL_HISTORY + tq, D_MODEL), F32),
            pltpu.VMEM((CONV_HISTORY + tq, SSM_CONV_DIM), F32),
            pltpu.VMEM((tq, SSM_CONV_DIM), F32),
            pltpu.VMEM((tq, SSM_INNER), F32),
            pltpu.VMEM((tq, D_MODEL), F32),
            pltpu.VMEM((SSM_GROUPS, SSM_STATE, SSM_GROUP_COLS), F32),
        ],
        compiler_params=pltpu.CompilerParams(
            dimension_semantics=("arbitrary", "arbitrary"),
            vmem_limit_bytes=VMEM_LIMIT_BYTES),
        name="mixer",
    )(*operands)


def _top16_rounds(scores, key_iota):
    cur = scores
    rank = jnp.full(scores.shape, float(PEER_TOPK), F32)
    tops = []
    for r in range(PEER_TOPK):
        m = jnp.max(cur, axis=0, keepdims=True)
        first = jnp.min(jnp.where(cur == m, key_iota, float(PEER_KEYS)), axis=0, keepdims=True)
        hit = key_iota == first
        cur = jnp.where(hit, -jnp.inf, cur)
        rank = jnp.where(hit, float(r), rank)
        tops.append(m)
    return rank, tops


def _sort_network_pairs(n):
    pairs = []
    p = 1
    while p < n:
        k = p
        while k >= 1:
            for j in range(k % p, n - k, 2 * k):
                for i in range(min(k, n - j - k)):
                    if (i + j) // (2 * p) == (i + j + k) // (2 * p):
                        pairs.append((i + j, i + j + k))
            k //= 2
        p *= 2
    return pairs


_SORT16 = _sort_network_pairs(PEER_TOPK)
F32_SUBLANES = 8


def _top16_sorted(scores):
    v = [scores[F32_SUBLANES * j:F32_SUBLANES * (j + 1)] for j in range(PEER_TOPK)]
    for (i, j) in _SORT16:
        v[i], v[j] = jnp.maximum(v[i], v[j]), jnp.minimum(v[i], v[j])
    shift = F32_SUBLANES // 2
    while shift >= 1:
        other = [pltpu.roll(x, shift, 0) for x in v]
        v = [jnp.maximum(v[j], other[PEER_TOPK - 1 - j]) for j in range(PEER_TOPK)]
        d = PEER_TOPK // 2
        while d >= 1:
            for j in range(PEER_TOPK):
                if (j // d) % 2 == 0:
                    v[j], v[j + d] = jnp.maximum(v[j], v[j + d]), jnp.minimum(v[j], v[j + d])
            d //= 2
        shift //= 2
    return v


def _route_kernel(h_ref, wqt_ref, keys_ref,
                  xt_ref, cnt_ref, e1_ref, rank2_ref, e2_ref,
                  qt_scr, key1_scr, e1_scr, tops_scr, csel_scr):
    tr = h_ref.shape[0]
    xt = h_ref[...].T.astype(BF16)
    xt_ref[...] = xt
    qt_scr[...] = _dot(wqt_ref[...], xt)

    key_iota = lax.broadcasted_iota(jnp.int32, (PEER_KEYS, LANES), 0).astype(F32)
    head_iota = lax.broadcasted_iota(jnp.int32, (PEER_HEADS, LANES), 0)
    tops_scr[...] = jnp.zeros(tops_scr.shape, F32)

    def stage1(exact_order):
        def head_body(h, tied):
            for k in range(2):
                q0 = pl.multiple_of((h * 2 + k) * PEER_HALF, PEER_HALF)
                qk = qt_scr[pl.ds(q0, PEER_HALF), :].astype(BF16)
                scores = _dot(keys_ref[k], qk)
                for lt in range(tr // LANES):
                    lanes = slice(lt * LANES, (lt + 1) * LANES)
                    sc = scores[:, lanes]
                    if exact_order:
                        rank, tops = _top16_rounds(sc, key_iota)
                        key1 = rank
                    else:
                        tops = _top16_sorted(sc)
                        at_least_last = jnp.where(sc >= tops[PEER_TOPK - 1][0:1], 1.0, 0.0)
                        n_sel = jnp.sum(at_least_last, axis=0, keepdims=True)
                        bad = jnp.abs(n_sel - float(PEER_TOPK))
                        for r in range(PEER_TOPK - 1):
                            bad = bad + jnp.where(tops[r][0:1] == tops[r + 1][0:1], 1.0, 0.0)
                        tied = jnp.maximum(tied, bad)
                        key1 = sc
                        if k == 1:
                            rank = jnp.full(sc.shape, float(PEER_TOPK), F32)
                            for r in reversed(range(PEER_TOPK)):
                                rank = jnp.where(sc >= tops[r][0:1], float(r), rank)
                    e = jnp.exp(sc - tops[0][0:1])
                    if k == 0:
                        key1_scr[h, :, lanes] = key1
                        e1_scr[h, :, lanes] = e
                    else:
                        rank2_ref[h, :, lanes] = rank.astype(BF16)
                        e2_ref[h, :, lanes] = e.astype(BF16)
                    for r in range(PEER_TOPK):
                        old = tops_scr[k, r, :, lanes]
                        tops_scr[k, r, :, lanes] = jnp.where(head_iota == h, tops[r], old)
            return tied

        return lax.fori_loop(0, PEER_HEADS, head_body, jnp.zeros((1, LANES), F32))

    any_tie = jnp.max(stage1(exact_order=False)) > 0.0

    @pl.when(any_tie)
    def _():
        stage1(exact_order=True)

    for lt in range(tr // LANES):
        lanes = slice(lt * LANES, (lt + 1) * LANES)
        s1 = [tops_scr[0, a, :, lanes] for a in range(PEER_TOPK)]
        s2 = [tops_scr[1, b, :, lanes] for b in range(PEER_TOPK)]
        sums = {ab: s1[ab[0]] + s2[ab[1]] for ab in _CANDIDATES}
        p1 = [jnp.exp(s1[a] - s1[0]) for a in range(PEER_TOPK)]
        p2 = [jnp.exp(s2[b] - s2[0]) for b in range(PEER_TOPK)]
        counts = [jnp.zeros((PEER_HEADS, LANES), F32) for _ in range(PEER_TOPK)]
        z = jnp.zeros((PEER_HEADS, LANES), F32)
        lost_late = {ab: 0.0 for ab in _CANDIDATES}
        won_early = {ab: 0.0 for ab in _CANDIDATES}
        n_later = {ab: 0 for ab in _CANDIDATES}
        for early in _CANDIDATES:
            for late in _CANDIDATES:
                if early[0] < late[0] and early[1] > late[1]:
                    early_wins = jnp.where(sums[early] >= sums[late], 1.0, 0.0)
                    lost_late[late] = lost_late[late] + early_wins
                    won_early[early] = won_early[early] + early_wins
                    n_later[early] += 1
        for (a, b) in _CANDIDATES:
            always = float((a + 1) * (b + 1) - 1 + n_later[(a, b)])
            beaten_by = always + lost_late[(a, b)] - won_early[(a, b)]
            sel = beaten_by < float(PEER_TOPK)
            counts[a] = counts[a] + jnp.where(sel, 1.0, 0.0)
            z = z + jnp.where(sel, p1[a] * p2[b], 0.0)
        for a in range(PEER_TOPK):
            csel_scr[a, :, lanes] = counts[a]
        csel_scr[PEER_TOPK, :, lanes] = 1.0 / z

    def stage3(keys_are_ranks):
        for h in range(PEER_HEADS):
            for lt in range(tr // LANES):
                lanes = slice(lt * LANES, (lt + 1) * LANES)
                key1 = key1_scr[h, :, lanes]
                cnt = jnp.zeros((PEER_KEYS, LANES), F32)
                for a in range(PEER_TOPK):
                    target = float(a) if keys_are_ranks else tops_scr[0, a, h:h + 1, lanes]
                    cnt = jnp.where(key1 == target, csel_scr[a, h:h + 1, lanes], cnt)
                cnt_ref[h, :, lanes] = cnt
                e1_ref[h, :, lanes] = e1_scr[h, :, lanes] * csel_scr[PEER_TOPK, h:h + 1, lanes]

    @pl.when(any_tie)
    def _():
        stage3(keys_are_ranks=True)

    @pl.when(jnp.logical_not(any_tie))
    def _():
        stage3(keys_are_ranks=False)


def _route(h1, w_q, sub_keys):
    t = h1.shape[0]
    tr = min(ROUTE_TOKENS, t)
    assert t % tr == 0 and tr % LANES == 0
    wqt = w_q.T.astype(BF16)
    keys = sub_keys.astype(BF16)
    nq = wqt.shape[0]
    table = lambda dt: jax.ShapeDtypeStruct((PEER_HEADS, PEER_KEYS, t), dt)
    table_spec = pl.BlockSpec((PEER_HEADS, PEER_KEYS, tr), lambda i: (0, 0, i))
    return pl.pallas_call(
        _route_kernel,
        grid=(t // tr,),
        in_specs=[
            pl.BlockSpec((tr, D_MODEL), lambda i: (i, 0)),
            pl.BlockSpec(wqt.shape, lambda i: (0, 0), pipeline_mode=pl.Buffered(1)),
            pl.BlockSpec(keys.shape, lambda i: (0, 0, 0), pipeline_mode=pl.Buffered(1)),
        ],
        out_specs=[
            pl.BlockSpec((D_MODEL, tr), lambda i: (0, i)),
            table_spec, table_spec, table_spec, table_spec,
        ],
        out_shape=[
            jax.ShapeDtypeStruct((D_MODEL, t), BF16),
            table(F32), table(F32), table(BF16), table(BF16),
        ],
        scratch_shapes=[
            pltpu.VMEM((nq, tr), F32),
            pltpu.VMEM((PEER_HEADS, PEER_KEYS, tr), F32),
            pltpu.VMEM((PEER_HEADS, PEER_KEYS, tr), F32),
            pltpu.VMEM((2, PEER_TOPK, PEER_HEADS, tr), F32),
            pltpu.VMEM((PEER_TOPK + 1, PEER_HEADS, tr), F32),
        ],
        compiler_params=pltpu.CompilerParams(
            dimension_semantics=("arbitrary",),
            vmem_limit_bytes=VMEM_LIMIT_BYTES),
        name="route",
    )(h1, wqt, keys)


def _expert_kernel(xt_ref, cnt_ref, e1_ref, rank2_ref, e2_ref, u_ref, vt_ref,
                   h1_ref, ln2g_ref, ln2b_ref,
                   out_ref,
                   acc_scr, act_scr, pre_a, pre_b):
    e_idx = pl.program_id(1)
    n_blocks = pl.num_programs(1) - 1
    sqrt_half = math.sqrt(0.5)

    def pre_activations(dst):
        dst[...] = _dot(u_ref[...], xt_ref[...]).astype(BF16)

    def row_to_keys(row):
        tile = jnp.broadcast_to(row, (BF16_SUBLANES, row.shape[1])).astype(BF16)
        reps = PEER_KEYS // BF16_SUBLANES
        return jnp.broadcast_to(tile[None], (reps,) + tile.shape).reshape(PEER_KEYS, row.shape[1])

    def finish_block(src):
        for i in range(EXPERT_ROWS):
            rows = slice(i * PEER_KEYS, (i + 1) * PEER_KEYS)
            hi = src[rows, :]
            gelu = 0.5 * hi * (1.0 + lax.erf(hi * sqrt_half))
            w = None
            for h in range(PEER_HEADS):
                cnt = row_to_keys(cnt_ref[h, i:i + 1, :])
                e1 = row_to_keys(e1_ref[h, i:i + 1, :])
                term = jnp.where(rank2_ref[h] < cnt, e2_ref[h], jnp.zeros((), BF16)) * e1
                w = term if w is None else w + term
            act_scr[rows, :] = gelu * w
        return _dot(vt_ref[...], act_scr[...])

    @pl.when(e_idx == 0)
    def _():
        acc_scr[...] = jnp.zeros(acc_scr.shape, F32)
        pre_activations(pre_a)

    for parity, (dst, src) in enumerate(((pre_a, pre_b), (pre_b, pre_a))):
        @pl.when((e_idx > 0) & (e_idx < n_blocks) & (e_idx % 2 == parity))
        def _(dst=dst, src=src):
            pre_activations(dst)
            acc_scr[...] += finish_block(src)

    @pl.when(e_idx == n_blocks)
    def _():
        total = acc_scr[...] + finish_block(pre_a if EXPERT_LAST_IN_A else pre_b)
        res = DN_ALPHA * h1_ref[...] + total.T
        out_ref[...] = _layer_norm(res, ln2g_ref[...], ln2b_ref[...])


def _experts(h1, xt, cnt, e1, rank2, e2, expert_u, expert_v, ln2_g, ln2_b):
    t = h1.shape[0]
    tm = min(EXPERT_TOKENS, t)
    assert t % tm == 0
    n_exp = expert_u.shape[0]
    eb = EXPERT_ROWS * PEER_KEYS
    assert n_exp == EXPERT_BLOCKS * eb
    u = expert_u.astype(BF16)
    vt = expert_v.astype(BF16).reshape(EXPERT_BLOCKS, eb, D_MODEL).transpose(0, 2, 1)
    last = EXPERT_BLOCKS - 1
    prev = lambda e: jnp.maximum(e - 1, 0)
    tab_full = pl.BlockSpec((PEER_HEADS, PEER_KEYS, tm), lambda i, e: (0, 0, i))
    tab_rows = pl.BlockSpec((PEER_HEADS, EXPERT_ROWS, tm), lambda i, e: (0, prev(e), i))
    vec = lambda v: v.reshape(1, -1).astype(F32)
    return pl.pallas_call(
        _expert_kernel,
        grid=(t // tm, EXPERT_BLOCKS + 1),
        in_specs=[
            pl.BlockSpec((D_MODEL, tm), lambda i, e: (0, i)),
            tab_rows, tab_rows, tab_full, tab_full,
            pl.BlockSpec((eb, D_MODEL), lambda i, e: (jnp.minimum(e, last), 0)),
            pl.BlockSpec((None, D_MODEL, eb), lambda i, e: (prev(e), 0, 0)),
            pl.BlockSpec((tm, D_MODEL), lambda i, e: (i, 0)),
            pl.BlockSpec((1, D_MODEL), lambda i, e: (0, 0)),
            pl.BlockSpec((1, D_MODEL), lambda i, e: (0, 0)),
        ],
        out_specs=pl.BlockSpec((tm, D_MODEL), lambda i, e: (i, 0)),
        out_shape=jax.ShapeDtypeStruct((t, D_MODEL), F32),
        scratch_shapes=[
            pltpu.VMEM((D_MODEL, tm), F32),
            pltpu.VMEM((eb, tm), BF16),
            pltpu.VMEM((eb, tm), BF16),
            pltpu.VMEM((eb, tm), BF16),
        ],
        compiler_params=pltpu.CompilerParams(
            dimension_semantics=("arbitrary", "arbitrary"),
            vmem_limit_bytes=VMEM_LIMIT_BYTES),
        name="experts",
    )(xt, cnt, e1, rank2, e2, u, vt, h1, vec(ln2_g), vec(ln2_b))


def kernel(x, w_in, pool_w, pool_scale, conv_w, conv_b, dt_bias, a_log, d_skip, ssm_norm_w,
           ssm_out, w_out, ln1_g, ln1_b, w_q, sub_keys, expert_u, expert_v, ln2_g, ln2_b):
    bsz, seq, d = x.shape
    h = x
    for i in range(DEPTH):
        h1 = _mixer(h, w_in[i], pool_w[i], pool_scale[i], conv_w[i], conv_b[i], dt_bias[i],
                    a_log[i], d_skip[i], ssm_norm_w[i], ssm_out[i], w_out[i], ln1_g[i], ln1_b[i])
        h1 = h1.reshape(bsz * seq, d)
        xt, cnt, e1, rank2, e2 = _route(h1, w_q[i], sub_keys[i])
        h = _experts(h1, xt, cnt, e1, rank2, e2, expert_u[i], expert_v[i], ln2_g[i], ln2_b[i])
        h = h.reshape(bsz, seq, d)
    return h
```
